```python
import jax, jax.numpy as jnp
from jax import lax
import numpy as np

D_MODEL = 1024
BATCH = 4
SEQ = 4096
DEPTH = 1

N_META = 16
BLOCK_Q = 128
EPS = 1e-6
ROPE_THETA = 10000.0
ATT_HEADS = 8
Q_LORA = 256
KV_LORA = 128
QK_NOPE = 64
QK_ROPE = 32
V_HEAD = 64
D_ATT = ATT_HEADS * V_HEAD
D_CONV = D_MODEL - D_ATT
CONV_WIDTH = 3
D_MIX = D_ATT + D_CONV
IN_COLS = Q_LORA + KV_LORA + QK_ROPE + 3 * D_CONV
PEER_HEADS = 8
N_KEYS = 128
N_EXPERTS = N_KEYS * N_KEYS
D_KEY = 256
PEER_TOPK = 16
PEER_BLOCK = 64

kernel_name = 'hymba_mla_shortconv_peer_layer'


def rms_norm(x, g):
    xf = x.astype(jnp.float32)
    y = xf * lax.rsqrt(jnp.mean(xf * xf, axis=-1, keepdims=True) + EPS)
    return (y * g.astype(jnp.float32)).astype(x.dtype)


def rope_tables(L):
    inv = 1.0 / (ROPE_THETA ** (jnp.arange(0, QK_ROPE, 2, dtype=jnp.float32) / QK_ROPE))
    ang = jnp.arange(L, dtype=jnp.float32)[:, None] * inv[None, :]
    return jnp.cos(ang), jnp.sin(ang)


def apply_rope(x, cos, sin):
    xf = x.astype(jnp.float32)
    x1, x2 = xf[..., :QK_ROPE // 2], xf[..., QK_ROPE // 2:]
    return jnp.concatenate([x1 * cos - x2 * sin, x1 * sin + x2 * cos], axis=-1).astype(x.dtype)


def mla_attention(c_q, c_kv, k_rope, q_norm_g, w_uq, kv_norm_g, w_ukv, cos, sin):
    B, L, _ = c_q.shape
    q = jnp.einsum('blr,rf->blf', rms_norm(c_q, q_norm_g), w_uq).reshape(B, L, ATT_HEADS, QK_NOPE + QK_ROPE)
    q_nope = q[..., :QK_NOPE]
    q_rope = apply_rope(q[..., QK_NOPE:], cos[None, :, None], sin[None, :, None])
    kv = jnp.einsum('blr,rf->blf', rms_norm(c_kv, kv_norm_g), w_ukv).reshape(B, L, ATT_HEADS, QK_NOPE + V_HEAD)
    k_nope, v = kv[..., :QK_NOPE], kv[..., QK_NOPE:]
    k_rope = apply_rope(k_rope, cos[None], sin[None])
    q = jnp.concatenate([q_nope, q_rope], axis=-1)
    k = jnp.concatenate([k_nope, jnp.broadcast_to(k_rope[:, :, None], (B, L, ATT_HEADS, QK_ROPE))], axis=-1)
    scale = (QK_NOPE + QK_ROPE) ** -0.5
    n_blk = L // BLOCK_Q
    q_blocks = q.reshape(B, n_blk, BLOCK_Q, ATT_HEADS, QK_NOPE + QK_ROPE).swapaxes(0, 1)
    k_pos = jnp.arange(L)
    neg = jnp.finfo(jnp.float32).min

    def one_block(args):
        qb, i = args
        s = jnp.einsum('bqhd,bkhd->bhqk', qb, k).astype(jnp.float32) * scale
        q_pos = i * BLOCK_Q + jnp.arange(BLOCK_Q)
        s = jnp.where(k_pos[None, :] <= q_pos[:, None], s, neg)
        p = jax.nn.softmax(s, axis=-1).astype(v.dtype)
        return jnp.einsum('bhqk,bkhd->bqhd', p, v)

    out = lax.map(one_block, (q_blocks, jnp.arange(n_blk)))
    return out.swapaxes(0, 1).reshape(B, L, D_ATT)


def short_conv(b_gate, c_gate, u, conv_w, conv_b):
    z = c_gate * u
    zp = jnp.pad(z, ((0, 0), (CONV_WIDTH - 1, 0), (0, 0)))
    L = z.shape[1]
    conv = conv_b + conv_w[0] * zp[:, 0:L] + conv_w[1] * zp[:, 1:L + 1] + conv_w[2] * zp[:, 2:L + 2]
    return b_gate * conv


def peer_ffn(x, w_query, keys1, keys2, u, v):
    B, L, _ = x.shape
    q = jnp.einsum('bld,df->blf', x, w_query).reshape(B, L, PEER_HEADS, 2, D_KEY // 2)
    s1 = jnp.einsum('blhd,nd->blhn', q[..., 0, :], keys1).astype(jnp.float32)
    s2 = jnp.einsum('blhd,nd->blhn', q[..., 1, :], keys2).astype(jnp.float32)
    v1, i1 = lax.top_k(s1, PEER_TOPK)
    v2, i2 = lax.top_k(s2, PEER_TOPK)
    cand = (v1[..., :, None] + v2[..., None, :]).reshape(B, L, PEER_HEADS, PEER_TOPK * PEER_TOPK)
    best, flat = lax.top_k(cand, PEER_TOPK)
    expert = (jnp.take_along_axis(i1, flat // PEER_TOPK, axis=-1) * N_KEYS
              + jnp.take_along_axis(i2, flat % PEER_TOPK, axis=-1))
    gate = jax.nn.softmax(best, axis=-1).astype(x.dtype)
    n_blk = L // PEER_BLOCK

    def blockify(a):
        return a.reshape(B, n_blk, PEER_BLOCK, *a.shape[2:]).swapaxes(0, 1)

    def one_block(args):
        xb, eb, gb = args
        ue = u[eb]
        ve = v[eb]
        act = jax.nn.gelu(jnp.einsum('bthkd,btd->bthk', ue, xb))
        return jnp.einsum('bthk,bthkd->btd', gb * act, ve)

    out = lax.map(one_block, (blockify(x), blockify(expert), blockify(gate)))
    return out.swapaxes(0, 1).reshape(B, L, D_MODEL)


def setup_inputs(seed: int = 0) -> dict:
    key = jax.random.key(seed)
    ks = jax.random.split(key, 24)
    f32 = jnp.float32

    def nrm(k, shape, scale):
        return jax.random.normal(k, shape, f32) * scale

    def gain(k, shape):
        return 1.0 + 0.02 * jax.random.normal(k, shape, f32)

    return {
        'x': nrm(ks[0], (BATCH, SEQ, D_MODEL), 1.0),
        'meta_tokens': nrm(ks[1], (N_META, D_MODEL), 1.0),
        'norm1_g': gain(ks[2], (DEPTH, D_MODEL)),
        'w_in': nrm(ks[3], (DEPTH, D_MODEL, IN_COLS), D_MODEL ** -0.5),
        'q_norm_g': gain(ks[4], (DEPTH, Q_LORA)),
        'w_uq': nrm(ks[5], (DEPTH, Q_LORA, ATT_HEADS * (QK_NOPE + QK_ROPE)), Q_LORA ** -0.5),
        'kv_norm_g': gain(ks[6], (DEPTH, KV_LORA)),
        'w_ukv': nrm(ks[7], (DEPTH, KV_LORA, ATT_HEADS * (QK_NOPE + V_HEAD)), KV_LORA ** -0.5),
        'conv_w': nrm(ks[8], (DEPTH, CONV_WIDTH, D_CONV), CONV_WIDTH ** -0.5),
        'conv_b': nrm(ks[9], (DEPTH, D_CONV), 0.01),
        'attn_out_g': gain(ks[10], (DEPTH, D_ATT)),
        'conv_out_g': gain(ks[11], (DEPTH, D_CONV)),
        'w_out': nrm(ks[12], (DEPTH, D_MIX, D_MODEL), D_MIX ** -0.5),
        'norm2_g': gain(ks[13], (DEPTH, D_MODEL)),
        'peer_w_query': nrm(ks[14], (DEPTH, D_MODEL, PEER_HEADS * D_KEY), D_MODEL ** -0.5),
        'peer_keys1': nrm(ks[15], (DEPTH, N_KEYS, D_KEY // 2), (D_KEY // 2) ** -0.5),
        'peer_keys2': nrm(ks[16], (DEPTH, N_KEYS, D_KEY // 2), (D_KEY // 2) ** -0.5),
        'peer_u': nrm(ks[17], (DEPTH, N_EXPERTS, D_MODEL), D_MODEL ** -0.5),
        'peer_v': nrm(ks[18], (DEPTH, N_EXPERTS, D_MODEL), PEER_HEADS ** -0.5),
        'final_g': gain(ks[19], (D_MODEL,)),
    }


def reference(x, meta_tokens, norm1_g, w_in, q_norm_g, w_uq, kv_norm_g, w_ukv, conv_w, conv_b,
              attn_out_g, conv_out_g, w_out, norm2_g, peer_w_query, peer_keys1, peer_keys2,
              peer_u, peer_v, final_g):
    B, S, _ = x.shape
    L = N_META + S
    L_pad = -(-L // BLOCK_Q) * BLOCK_Q
    meta = jnp.broadcast_to(meta_tokens[None].astype(x.dtype), (B, N_META, D_MODEL))
    h = jnp.concatenate([meta, x], axis=1)
    h = jnp.pad(h, ((0, 0), (0, L_pad - L), (0, 0)))
    cos, sin = rope_tables(L_pad)
    o1 = Q_LORA
    o2 = o1 + KV_LORA
    o3 = o2 + QK_ROPE
    o4 = o3 + D_CONV
    o5 = o4 + D_CONV
    for l in range(DEPTH):
        hn = rms_norm(h, norm1_g[l])
        p = jnp.einsum('bld,df->blf', hn, w_in[l])
        c_q, c_kv, k_rope = p[..., :o1], p[..., o1:o2], p[..., o2:o3]
        b_gate, c_gate, u = p[..., o3:o4], p[..., o4:o5], p[..., o5:]
        attn = mla_attention(c_q, c_kv, k_rope, q_norm_g[l], w_uq[l], kv_norm_g[l], w_ukv[l], cos, sin)
        conv = short_conv(b_gate, c_gate, u, conv_w[l], conv_b[l])
        mixed = jnp.concatenate([rms_norm(attn, attn_out_g[l]), rms_norm(conv, conv_out_g[l])], axis=-1)
        h = h + jnp.einsum('blf,fd->bld', mixed, w_out[l])
        h = h + peer_ffn(rms_norm(h, norm2_g[l]), peer_w_query[l], peer_keys1[l], peer_keys2[l],
                         peer_u[l], peer_v[l])
    return rms_norm(h, final_g)[:, N_META:L]
```

```python
import functools

import numpy as np
import jax
import jax.numpy as jnp
from jax import lax
from jax.experimental import pallas as pl
from jax.experimental.pallas import tpu as pltpu

F32 = jnp.float32
BF16 = jnp.bfloat16

N_META = 16
EPS = 1e-6
ROPE_THETA = 10000.0
ATT_HEADS = 8
Q_LORA = 256
KV_LORA = 128
QK_NOPE = 64
QK_ROPE = 32
V_HEAD = 64
D_ATT = ATT_HEADS * V_HEAD
PEER_HEADS = 8
N_KEYS = 128
D_KEY = 256
PEER_TOPK = 16
CONV_WIDTH = 3

LANES = 128
HEAD_SLAB = LANES
NEG_BIG = -1e30
LOG2E = 1.4426950408889634
VMEM_LIMIT = 56 * 1024 * 1024

TM_PROJ = 512
TQ_ATT = 512
TM_MIX = 512
T_PEER = 512
E_CHUNK = 1024


def _dot(a, b):
    return jnp.dot(a, b, preferred_element_type=F32)


def _dot_nt(a, b):
    return lax.dot_general(a, b, (((1,), (1,)), ((), ())), preferred_element_type=F32)


def _rms(x, g):
    y = x * lax.rsqrt(jnp.mean(x * x, axis=-1, keepdims=True) + EPS)
    return y * g


def _proj_kernel(x_ref, zprev_ref, g1_ref, win_ref, qg_ref, wq_ref, kvg_ref, wkv_ref, cw_ref, cb_ref,
                 cg_ref, cos_ref, sin_ref,
                 q_ref, k_ref, v_ref, cn_ref, zt_ref, zbuf, *, tm, d_conv, q_scale):
    t = pl.program_id(1)

    @pl.when(t == 0)
    def _():
        zbuf[0:8, :] = zprev_ref[...]

    x = x_ref[0]
    hn = _rms(x, g1_ref[...]).astype(BF16)
    p = _dot(hn, win_ref[...])
    cos = cos_ref[...]
    sin = sin_ref[...]

    qn = _rms(p[:, :Q_LORA], qg_ref[...]).astype(BF16)
    qq = _dot(qn, wq_ref[...])
    hw = ATT_HEADS * HEAD_SLAB
    for h in range(ATT_HEADS):
        lo = h * HEAD_SLAB
        qh = qq[:, lo:lo + HEAD_SLAB] * cos + qq[:, hw + lo:hw + lo + HEAD_SLAB] * sin
        q_ref[0, :, lo:lo + HEAD_SLAB] = (qh * q_scale).astype(BF16)

    o1 = Q_LORA
    o2 = o1 + KV_LORA
    kvn = _rms(p[:, o1:o2], kvg_ref[...]).astype(BF16)
    kv = _dot(kvn, wkv_ref[...])
    kr = p[:, o2:o2 + HEAD_SLAB] * cos + p[:, o2 + HEAD_SLAB:o2 + 2 * HEAD_SLAB] * sin
    for h in range(ATT_HEADS):
        lo = h * HEAD_SLAB
        k_ref[0, :, lo:lo + HEAD_SLAB] = (kv[:, lo:lo + HEAD_SLAB] + kr).astype(BF16)
    v_ref[0] = kv[:, hw:].astype(BF16)

    o3 = o2 + 2 * HEAD_SLAB
    bg = p[:, o3:o3 + d_conv]
    z = p[:, o3 + d_conv:o3 + 2 * d_conv] * p[:, o3 + 2 * d_conv:o3 + 3 * d_conv]
    zbuf[8:8 + tm, :] = z
    conv = (cb_ref[...] + cw_ref[0:1, :] * zbuf[6:6 + tm, :] + cw_ref[1:2, :] * zbuf[7:7 + tm, :]
            + cw_ref[2:3, :] * z)
    cn_ref[0] = _rms(bg * conv, cg_ref[...]).astype(BF16)
    tail = zbuf[tm:tm + 8, :]
    zt_ref[0] = tail
    zbuf[0:8, :] = tail


def _proj_call(x, zprev, w, cos, sin, tm):
    b, s, d = x.shape
    d_conv = w['conv_b'].shape[-1]
    hw = ATT_HEADS * HEAD_SLAB
    nt = s // tm
    q_scale = float((QK_NOPE + QK_ROPE) ** -0.5 * LOG2E)
    const = lambda bi, ti: (0, 0)
    kern = functools.partial(_proj_kernel, tm=tm, d_conv=d_conv, q_scale=q_scale)
    return pl.pallas_call(
        kern,
        name="proj",
        grid=(b, nt),
        in_specs=[
            pl.BlockSpec((1, tm, d), lambda bi, ti: (bi, ti, 0)),
            pl.BlockSpec((8, d_conv), const),
            pl.BlockSpec((1, d), const),
            pl.BlockSpec(w['w_in'].shape, const),
            pl.BlockSpec((1, Q_LORA), const),
            pl.BlockSpec(w['w_q'].shape, const),
            pl.BlockSpec((1, KV_LORA), const),
            pl.BlockSpec(w['w_kv'].shape, const),
            pl.BlockSpec((CONV_WIDTH, d_conv), const),
            pl.BlockSpec((1, d_conv), const),
            pl.BlockSpec((1, d_conv), const),
            pl.BlockSpec((tm, HEAD_SLAB), lambda bi, ti: (ti, 0)),
            pl.BlockSpec((tm, HEAD_SLAB), lambda bi, ti: (ti, 0)),
        ],
        out_specs=[
            pl.BlockSpec((1, tm, hw), lambda bi, ti: (bi, ti, 0)),
            pl.BlockSpec((1, tm, hw), lambda bi, ti: (bi, ti, 0)),
            pl.BlockSpec((1, tm, hw), lambda bi, ti: (bi, ti, 0)),
            pl.BlockSpec((1, tm, d_conv), lambda bi, ti: (bi, ti, 0)),
            pl.BlockSpec((1, 8, d_conv), lambda bi, ti: (bi, ti, 0)),
        ],
        out_shape=[
            jax.ShapeDtypeStruct((b, s, hw), BF16),
            jax.ShapeDtypeStruct((b, s, hw), BF16),
            jax.ShapeDtypeStruct((b, s, hw), BF16),
            jax.ShapeDtypeStruct((b, s, d_conv), BF16),
            jax.ShapeDtypeStruct((b, nt * 8, d_conv), F32),
        ],
        scratch_shapes=[pltpu.VMEM((tm + 8, d_conv), F32)],
        compiler_params=pltpu.CompilerParams(
            dimension_semantics=("arbitrary", "arbitrary"), vmem_limit_bytes=VMEM_LIMIT),
    )(x, zprev, w['norm1_g'], w['w_in'], w['q_norm_g'], w['w_q'], w['kv_norm_g'], w['w_kv'],
      w['conv_w'], w['conv_b'], w['conv_out_g'], cos, sin)


def _attn_kernel(q_ref, k_ref, v_ref, km_ref, vm_ref, o_ref, *, tq):
    i = pl.program_id(2)
    row = lax.broadcasted_iota(jnp.int32, (tq, tq), 0)
    col = lax.broadcasted_iota(jnp.int32, (tq, tq), 1)
    causal = col <= row
    meta_ok = lax.broadcasted_iota(jnp.int32, (tq, LANES), 1) < N_META

    def step(carry, qh, kc, vc, mask):
        m, l, acc = carry
        s = _dot_nt(qh, kc)
        if mask is not None:
            s = jnp.where(mask, s, NEG_BIG)
        m_new = jnp.maximum(m, jnp.max(s, axis=-1, keepdims=True))
        alpha = jnp.exp2(m - m_new)
        pr = jnp.exp2(s - m_new)
        l = alpha * l + jnp.sum(pr, axis=-1, keepdims=True)
        acc = alpha * acc + _dot(pr.astype(BF16), vc)
        return m_new, l, acc

    qs = [q_ref[0, :, hh * HEAD_SLAB:(hh + 1) * HEAD_SLAB] for hh in range(2)]
    init = (jnp.full((tq, 1), NEG_BIG, F32), jnp.zeros((tq, 1), F32), jnp.zeros((tq, LANES), F32))
    carries = []
    for hh in range(2):
        lo = hh * HEAD_SLAB
        carries.append(step(init, qs[hh], km_ref[:, lo:lo + HEAD_SLAB], vm_ref[:, lo:lo + HEAD_SLAB],
                            meta_ok))

    def body(j, cs):
        off = pl.multiple_of(j * tq, tq)
        out = []
        for hh in range(2):
            lo = hh * HEAD_SLAB
            out.append(step(cs[hh], qs[hh], k_ref[0, pl.ds(off, tq), lo:lo + HEAD_SLAB],
                            v_ref[0, pl.ds(off, tq), lo:lo + HEAD_SLAB], None))
        return tuple(out)

    carries = lax.fori_loop(0, i, body, tuple(carries))
    off = pl.multiple_of(i * tq, tq)
    o = None
    for hh in range(2):
        lo = hh * HEAD_SLAB
        m, l, acc = step(carries[hh], qs[hh], k_ref[0, pl.ds(off, tq), lo:lo + HEAD_SLAB],
                         v_ref[0, pl.ds(off, tq), lo:lo + HEAD_SLAB], causal)
        oh = acc / l
        o = oh if o is None else o + oh
    o_ref[0] = o.astype(BF16)


def _attn_call(q, k, v, km, vm, tq):
    b, s, hw = q.shape
    npair = ATT_HEADS // 2
    pw = 2 * HEAD_SLAB
    kern = functools.partial(_attn_kernel, tq=tq)
    return pl.pallas_call(
        kern,
        name="attn",
        grid=(b, npair, s // tq),
        in_specs=[
            pl.BlockSpec((1, tq, pw), lambda bi, pi, qi: (bi, qi, pi)),
            pl.BlockSpec((1, s, pw), lambda bi, pi, qi: (bi, 0, pi)),
            pl.BlockSpec((1, s, pw), lambda bi, pi, qi: (bi, 0, pi)),
            pl.BlockSpec((LANES, pw), lambda bi, pi, qi: (0, pi)),
            pl.BlockSpec((LANES, pw), lambda bi, pi, qi: (0, pi)),
        ],
        out_specs=pl.BlockSpec((1, tq, LANES), lambda bi, pi, qi: (bi, qi, pi)),
        out_shape=jax.ShapeDtypeStruct((b, s, D_ATT), BF16),
        compiler_params=pltpu.CompilerParams(
            dimension_semantics=("arbitrary", "arbitrary", "arbitrary"), vmem_limit_bytes=VMEM_LIMIT),
    )(q, k, v, km, vm)


def _top16_tile(s):
    key = lax.broadcasted_iota(jnp.int32, s.shape, 0)
    rank = jnp.full(s.shape, float(PEER_TOPK), F32)
    vals = []
    for r in range(PEER_TOPK):
        m = jnp.max(s, axis=0, keepdims=True)
        first = jnp.min(jnp.where(s == m, key, N_KEYS), axis=0, keepdims=True)
        hit = key == first
        rank = jnp.where(hit, float(r), rank)
        s = jnp.where(hit, -jnp.inf, s)
        vals.append(m)
    return rank, jnp.concatenate(vals, axis=0)


def _candidate_blocks():
    blocks = []
    for a in range(8):
        nb = PEER_TOPK // (a + 1)
        for b0 in range(0, nb, 8):
            nv = min(8, nb - b0)
            flat = np.array([a * PEER_TOPK + b0 + r for r in range(8)])
            blocks.append(('row', a, b0, nv, flat))
    flat = np.array([(8 + r) * PEER_TOPK for r in range(8)])
    blocks.append(('col', 8, 0, 8, flat))
    return blocks


_CAND_BLOCKS = _candidate_blocks()


def _select_counts(v1, v2):
    sub = lax.broadcasted_iota(jnp.int32, (8, LANES), 0)
    vals = []
    for kind, a0, b0, nv, _ in _CAND_BLOCKS:
        if kind == 'row':
            blk = v1[a0:a0 + 1, :] + v2[b0:b0 + 8, :]
        else:
            blk = v1[a0:a0 + 8, :] + v2[b0:b0 + 1, :]
        if nv < 8:
            blk = jnp.where(sub < nv, blk, -jnp.inf)
        vals.append(blk)
    beaten = [jnp.zeros((8, LANES), F32) for _ in vals]
    for si, (_, _, _, nv_s, flat_s) in enumerate(_CAND_BLOCKS):
        for r in range(nv_s):
            src = jnp.broadcast_to(vals[si][r:r + 1, :], (8, LANES))
            for ti, (_, _, _, _, flat_t) in enumerate(_CAND_BLOCKS):
                wins_tie = flat_s[r] < flat_t
                gt = jnp.where(src > vals[ti], 1.0, 0.0)
                ge = jnp.where(src >= vals[ti], 1.0, 0.0)
                if wins_tie.all():
                    beats = ge
                elif not wins_tie.any():
                    beats = gt
                else:
                    beats = jnp.where(sub >= int((~wins_tie).sum()), ge, gt)
                beaten[ti] = beaten[ti] + beats
    top = vals[0][0:1, :]
    z = jnp.zeros((1, LANES), F32)
    cnt_rows = [jnp.zeros((1, LANES), F32) for _ in range(PEER_TOPK)]
    for ti, (kind, a0, b0, nv, _) in enumerate(_CAND_BLOCKS):
        self_f = jnp.where(beaten[ti] < float(PEER_TOPK), 1.0, 0.0)
        if nv < 8:
            self_f = jnp.where(sub < nv, self_f, 0.0)
        z = z + jnp.sum(self_f * jnp.exp(vals[ti] - top), axis=0, keepdims=True)
        if kind == 'row':
            cnt_rows[a0] = cnt_rows[a0] + jnp.sum(self_f, axis=0, keepdims=True)
        else:
            for r in range(8):
                cnt_rows[a0 + r] = cnt_rows[a0 + r] + self_f[r:r + 1, :]
    return jnp.concatenate(cnt_rows, axis=0), z


def _mix_kernel(att_ref, cn_ref, x_ref, ag_ref, wo_ref, g2_ref, wqr_ref, k1_ref, k2_ref,
                h1_ref, xn_ref, c_ref, e1_ref, r2_ref, e2_ref, st, rk, vs, *, tm):
    nlt = tm // LANES
    att = _rms(att_ref[...].astype(F32), ag_ref[...]).astype(BF16)
    d_att = att.shape[-1]
    h1 = x_ref[...] + _dot(att, wo_ref[0:d_att, :]) + _dot(cn_ref[...], wo_ref[d_att:, :])
    h1_ref[...] = h1
    xn = _rms(h1, g2_ref[...]).astype(BF16)
    xn_ref[...] = xn
    qp = _dot(xn, wqr_ref[...]).astype(BF16)
    half = D_KEY // 2
    for h in range(PEER_HEADS):
        for side, kref in enumerate((k1_ref, k2_ref)):
            lo = h * D_KEY + side * half
            sc = _dot_nt(kref[...], qp[:, lo:lo + half])
            for lt in range(nlt):
                st[2 * h + side, lt] = sc[:, lt * LANES:(lt + 1) * LANES]

    def rank_body(idx, carry):
        p = idx // nlt
        lt = idx % nlt
        rank, vals = _top16_tile(st[p, lt])
        rk[p, lt] = rank
        vs[p, lt] = vals
        return carry

    lax.fori_loop(0, 2 * PEER_HEADS * nlt, rank_body, 0)

    def sel_body(idx, carry):
        h = idx // nlt
        lt = idx % nlt
        v1 = vs[2 * h, lt]
        v2 = vs[2 * h + 1, lt]
        cnt, z = _select_counts(v1, v2)
        r1 = rk[2 * h, lt]
        c = jnp.zeros((N_KEYS, LANES), F32)
        for a in range(PEER_TOPK):
            c = c + jnp.where(r1 == float(a), cnt[a:a + 1, :], 0.0)
        c_ref[h, lt] = c
        e1_ref[h, lt] = jnp.exp(st[2 * h, lt] - v1[0:1, :]) * (1.0 / z)
        r2_ref[h, lt] = rk[2 * h + 1, lt].astype(BF16)
        e2_ref[h, lt] = jnp.exp(st[2 * h + 1, lt] - v2[0:1, :]).astype(BF16)
        return carry

    lax.fori_loop(0, PEER_HEADS * nlt, sel_body, 0)


def _mix_call(att, cn, x2, w, tm):
    n, d = x2.shape
    nlt = tm // LANES
    d_mix = w['w_out'].shape[0]
    const = lambda ti: (0, 0)
    tile4 = pl.BlockSpec((PEER_HEADS, nlt, N_KEYS, LANES), lambda ti: (0, ti, 0, 0))
    sel_shape = (PEER_HEADS, n // LANES, N_KEYS, LANES)
    kern = functools.partial(_mix_kernel, tm=tm)
    return pl.pallas_call(
        kern,
        name="mix",
        grid=(n // tm,),
        in_specs=[
            pl.BlockSpec((tm, att.shape[-1]), lambda ti: (ti, 0)),
            pl.BlockSpec((tm, cn.shape[-1]), lambda ti: (ti, 0)),
            pl.BlockSpec((tm, d), lambda ti: (ti, 0)),
            pl.BlockSpec((1, att.shape[-1]), const),
            pl.BlockSpec((d_mix, d), const),
            pl.BlockSpec((1, d), const),
            pl.BlockSpec(w['w_query'].shape, const),
            pl.BlockSpec((N_KEYS, D_KEY // 2), const),
            pl.BlockSpec((N_KEYS, D_KEY // 2), const),
        ],
        out_specs=[
            pl.BlockSpec((tm, d), lambda ti: (ti, 0)),
            pl.BlockSpec((tm, d), lambda ti: (ti, 0)),
            tile4, tile4, tile4, tile4,
        ],
        out_shape=[
            jax.ShapeDtypeStruct((n, d), F32),
            jax.ShapeDtypeStruct((n, d), BF16),
            jax.ShapeDtypeStruct(sel_shape, F32),
            jax.ShapeDtypeStruct(sel_shape, F32),
            jax.ShapeDtypeStruct(sel_shape, BF16),
            jax.ShapeDtypeStruct(sel_shape, BF16),
        ],
        scratch_shapes=[
            pltpu.VMEM((2 * PEER_HEADS, nlt, N_KEYS, LANES), F32),
            pltpu.VMEM((2 * PEER_HEADS, nlt, N_KEYS, LANES), F32),
            pltpu.VMEM((2 * PEER_HEADS, nlt, PEER_TOPK, LANES), F32),
        ],
        compiler_params=pltpu.CompilerParams(
            dimension_semantics=("arbitrary",), vmem_limit_bytes=VMEM_LIMIT),
    )(att, cn, x2, w['attn_out_g'], w['w_out'], w['norm2_g'], w['w_query'], w['keys1'], w['keys2'])


def _gelu_tanh(x):
    c = float(np.sqrt(2.0 / np.pi))
    return 0.5 * x * (1.0 + jnp.tanh(c * (x + 0.044715 * (x * x * x))))


def _peer_kernel(xn_ref, u_ref, vt_ref, c_ref, e1_ref, r2_ref, e2_ref, h1_ref, fg_ref, y_ref,
                 acc, pbuf, *, t, ec):
    e = pl.program_id(1)
    nlt = t // LANES
    nib = ec // N_KEYS

    @pl.when(e == 0)
    def _():
        acc[...] = jnp.zeros_like(acc)

    act = _dot_nt(u_ref[...], xn_ref[...])
    row0 = pl.multiple_of(e * nib, nib)
    for lt in range(nlt):
        ls = slice(lt * LANES, (lt + 1) * LANES)
        cs = [c_ref[h, lt, pl.ds(row0, nib), :] for h in range(PEER_HEADS)]
        es = [e1_ref[h, lt, pl.ds(row0, nib), :] for h in range(PEER_HEADS)]
        for ib in range(nib):
            g = _gelu_tanh(act[ib * N_KEYS:(ib + 1) * N_KEYS, ls]).astype(BF16)
            gate = jnp.zeros((N_KEYS, LANES), BF16)
            for h in range(PEER_HEADS):
                cb = jnp.broadcast_to(cs[h][ib:ib + 1, :], (N_KEYS, LANES)).astype(BF16)
                eb = jnp.broadcast_to(es[h][ib:ib + 1, :], (N_KEYS, LANES)).astype(BF16)
                gate = gate + jnp.where(r2_ref[h, lt] < cb, e2_ref[h, lt] * eb, jnp.zeros((), BF16))
            pbuf[ib * N_KEYS:(ib + 1) * N_KEYS, ls] = gate * g
    acc[...] += _dot(vt_ref[...], pbuf[...])

    @pl.when(e == pl.num_programs(1) - 1)
    def _():
        h2 = h1_ref[...] + acc[...].T
        y_ref[...] = _rms(h2, fg_ref[...])


def _peer_call(xn, u, vt, c, e1, r2, e2, h1, fg, t, ec):
    n, d = xn.shape
    ne = u.shape[0] // ec
    nlt = t // LANES
    tile4 = pl.BlockSpec((PEER_HEADS, nlt, N_KEYS, LANES), lambda ti, ei: (0, ti, 0, 0))
    kern = functools.partial(_peer_kernel, t=t, ec=ec)
    return pl.pallas_call(
        kern,
        name="peer",
        grid=(n // t, ne),
        in_specs=[
            pl.BlockSpec((t, d), lambda ti, ei: (ti, 0)),
            pl.BlockSpec((ec, d), lambda ti, ei: (ei, 0)),
            pl.BlockSpec((d, ec), lambda ti, ei: (0, ei)),
            tile4, tile4, tile4, tile4,
            pl.BlockSpec((t, d), lambda ti, ei: (ti, 0)),
            pl.BlockSpec((1, d), lambda ti, ei: (0, 0)),
        ],
        out_specs=pl.BlockSpec((t, d), lambda ti, ei: (ti, 0)),
        out_shape=jax.ShapeDtypeStruct((n, d), F32),
        scratch_shapes=[pltpu.VMEM((d, t), F32), pltpu.VMEM((ec, t), BF16)],
        compiler_params=pltpu.CompilerParams(
            dimension_semantics=("arbitrary", "arbitrary"), vmem_limit_bytes=VMEM_LIMIT),
    )(xn, u, vt, c, e1, r2, e2, h1, fg)


def _rope_slab_tables(length):
    inv = 1.0 / (ROPE_THETA ** (jnp.arange(0, QK_ROPE, 2, dtype=F32) / QK_ROPE))
    ang = jnp.arange(length, dtype=F32)[:, None] * inv[None, :]
    cos, sin = jnp.cos(ang), jnp.sin(ang)
    half = QK_ROPE // 2
    pad = HEAD_SLAB - QK_NOPE - QK_ROPE
    ones = jnp.ones((length, QK_NOPE), F32)
    zeros_n = jnp.zeros((length, QK_NOPE), F32)
    zeros_p = jnp.zeros((length, pad), F32)
    del half
    cos_t = jnp.concatenate([ones, cos, cos, zeros_p], axis=1)
    sin_t = jnp.concatenate([zeros_n, -sin, sin, zeros_p], axis=1)
    return cos_t, sin_t


def _head_slabs(wcols, n_heads, width, offset=0):
    k = wcols.shape[0]
    w3 = wcols.reshape(k, n_heads, width)
    out = jnp.zeros((k, n_heads, HEAD_SLAB), wcols.dtype)
    if isinstance(offset, int):
        return out.at[:, :, offset:offset + width].set(w3).reshape(k, n_heads * HEAD_SLAB)
    for h in range(n_heads):
        out = out.at[:, h, offset[h]:offset[h] + width].set(w3[:, h])
    return out.reshape(k, n_heads * HEAD_SLAB)


def _prep_weights(norm1_g, w_in, q_norm_g, w_uq, kv_norm_g, w_ukv, conv_w, conv_b, attn_out_g,
                  conv_out_g, w_out, norm2_g, peer_w_query, peer_keys1, peer_keys2):
    half = QK_ROPE // 2
    o1 = Q_LORA
    o2 = o1 + KV_LORA
    o3 = o2 + QK_ROPE
    k_in = w_in.shape[0]
    kr = w_in[:, o2:o3]
    zpad_n = jnp.zeros((k_in, QK_NOPE), F32)
    zpad_p = jnp.zeros((k_in, HEAD_SLAB - QK_NOPE - QK_ROPE), F32)
    kr_a = jnp.concatenate([zpad_n, kr[:, :half], kr[:, half:], zpad_p], axis=1)
    kr_b = jnp.concatenate([zpad_n, kr[:, half:], kr[:, :half], zpad_p], axis=1)
    w_in_ext = jnp.concatenate([w_in[:, :o2], kr_a, kr_b, w_in[:, o3:]], axis=1).astype(BF16)

    qd = QK_NOPE + QK_ROPE
    wq3 = w_uq.reshape(Q_LORA, ATT_HEADS, qd)
    wq_swap = jnp.concatenate([wq3[:, :, :QK_NOPE], wq3[:, :, QK_NOPE + half:], wq3[:, :, QK_NOPE:QK_NOPE + half]],
                              axis=2).reshape(Q_LORA, ATT_HEADS * qd)
    w_q = jnp.concatenate([_head_slabs(w_uq, ATT_HEADS, qd), _head_slabs(wq_swap, ATT_HEADS, qd)],
                          axis=1).astype(BF16)

    wkv3 = w_ukv.reshape(KV_LORA, ATT_HEADS, QK_NOPE + V_HEAD)
    wk = wkv3[:, :, :QK_NOPE].reshape(KV_LORA, ATT_HEADS * QK_NOPE)
    wv = wkv3[:, :, QK_NOPE:].reshape(KV_LORA, ATT_HEADS * V_HEAD)
    v_off = [V_HEAD * (h % 2) for h in range(ATT_HEADS)]
    w_kv = jnp.concatenate([_head_slabs(wk, ATT_HEADS, QK_NOPE), _head_slabs(wv, ATT_HEADS, V_HEAD, v_off)],
                           axis=1).astype(BF16)
    return {
        'norm1_g': norm1_g.reshape(1, -1), 'w_in': w_in_ext,
        'q_norm_g': q_norm_g.reshape(1, -1), 'w_q': w_q,
        'kv_norm_g': kv_norm_g.reshape(1, -1), 'w_kv': w_kv,
        'conv_w': conv_w, 'conv_b': conv_b.reshape(1, -1), 'conv_out_g': conv_out_g.reshape(1, -1),
        'attn_out_g': attn_out_g.reshape(1, -1), 'w_out': w_out.astype(BF16),
        'norm2_g': norm2_g.reshape(1, -1), 'w_query': peer_w_query.astype(BF16),
        'keys1': peer_keys1.astype(BF16), 'keys2': peer_keys2.astype(BF16),
    }


def kernel(x, meta_tokens, norm1_g, w_in, q_norm_g, w_uq, kv_norm_g, w_ukv, conv_w, conv_b, attn_out_g,
           conv_out_g, w_out, norm2_g, peer_w_query, peer_keys1, peer_keys2, peer_u, peer_v, final_g):
    b, s, d = x.shape
    assert norm1_g.shape[0] == 1, "single-layer kernel"
    assert meta_tokens.shape[0] == N_META
    tm = min(TM_PROJ, s)
    assert s % tm == 0 and s % LANES == 0
    w = _prep_weights(norm1_g[0], w_in[0], q_norm_g[0], w_uq[0], kv_norm_g[0], w_ukv[0], conv_w[0],
                      conv_b[0], attn_out_g[0], conv_out_g[0], w_out[0], norm2_g[0], peer_w_query[0],
                      peer_keys1[0], peer_keys2[0])
    d_conv = conv_b.shape[-1]
    cos_t, sin_t = _rope_slab_tables(N_META + s)

    _, k_m, v_m, _, z_m = _proj_call(meta_tokens[None], jnp.zeros((8, d_conv), F32), w,
                                     cos_t[:N_META], sin_t[:N_META], N_META)
    q, k, v, cn, _ = _proj_call(x, z_m[0], w, cos_t[N_META:], sin_t[N_META:], tm)
    hw = ATT_HEADS * HEAD_SLAB
    km = jnp.zeros((LANES, hw), BF16).at[:N_META].set(k_m[0])
    vm = jnp.zeros((LANES, hw), BF16).at[:N_META].set(v_m[0])
    att = _attn_call(q, k, v, km, vm, min(TQ_ATT, s))

    n = b * s
    tmix = min(TM_MIX, n)
    h1, xn, c, e1, r2, e2 = _mix_call(att.reshape(n, D_ATT), cn.reshape(n, d_conv), x.reshape(n, d), w, tmix)
    u_bf = peer_u[0].astype(BF16)
    vt_bf = peer_v[0].T.astype(BF16)
    y = _peer_call(xn, u_bf, vt_bf, c, e1, r2, e2, h1, final_g.reshape(1, -1), min(T_PEER, n), E_CHUNK)
    return y.reshape(b, s, d)
```

```python
import functools

import numpy as np
import jax
import jax.numpy as jnp
from jax import lax
from jax.experimental import pallas as pl
from jax.experimental.pallas import tpu as pltpu

F32 = jnp.float32
BF16 = jnp.bfloat16

N_META = 16
EPS = 1e-6
ROPE_THETA = 10000.0
ATT_HEADS = 8
Q_LORA = 256
KV_LORA = 128
QK_NOPE = 64
QK_ROPE = 32
V_HEAD = 64
D_ATT = ATT_HEADS * V_HEAD
PEER_HEADS = 8
N_KEYS = 128
D_KEY = 256
PEER_TOPK = 16
CONV_WIDTH = 3

LANES = 128
HEAD_SLAB = LANES
NEG_BIG = -1e30
LOG2E = 1.4426950408889634
VMEM_LIMIT = 56 * 1024 * 1024

TM_PROJ = 512
TQ_ATT = 512
TM_MIX = 512
T_PEER = 512
E_CHUNK = 1024
SUB_EXPERTS = 256


def _dot(a, b):
    return jnp.dot(a, b, preferred_element_type=F32)


def _dot_nt(a, b):
    return lax.dot_general(a, b, (((1,), (1,)), ((), ())), preferred_element_type=F32)


def _rms(x, g):
    y = x * lax.rsqrt(jnp.mean(x * x, axis=-1, keepdims=True) + EPS)
    return y * g


def _proj_kernel(x_ref, zprev_ref, g1_ref, win_ref, qg_ref, wq_ref, kvg_ref, wkv_ref, cw_ref, cb_ref,
                 cg_ref, cos_ref, sin_ref,
                 q_ref, k_ref, v_ref, cn_ref, zt_ref, zbuf, *, tm, d_conv, q_scale):
    t = pl.program_id(1)

    @pl.when(t == 0)
    def _():
        zbuf[0:8, :] = zprev_ref[...]

    x = x_ref[0]
    hn = _rms(x, g1_ref[...]).astype(BF16)
    p = _dot(hn, win_ref[...])
    cos = cos_ref[...]
    sin = sin_ref[...]

    qn = _rms(p[:, :Q_LORA], qg_ref[...]).astype(BF16)
    qq = _dot(qn, wq_ref[...])
    hw = ATT_HEADS * HEAD_SLAB
    for h in range(ATT_HEADS):
        lo = h * HEAD_SLAB
        qh = qq[:, lo:lo + HEAD_SLAB] * cos + qq[:, hw + lo:hw + lo + HEAD_SLAB] * sin
        q_ref[0, :, lo:lo + HEAD_SLAB] = (qh * q_scale).astype(BF16)

    o1 = Q_LORA
    o2 = o1 + KV_LORA
    kvn = _rms(p[:, o1:o2], kvg_ref[...]).astype(BF16)
    kv = _dot(kvn, wkv_ref[...])
    kr = p[:, o2:o2 + HEAD_SLAB] * cos + p[:, o2 + HEAD_SLAB:o2 + 2 * HEAD_SLAB] * sin
    for h in range(ATT_HEADS):
        lo = h * HEAD_SLAB
        k_ref[0, :, lo:lo + HEAD_SLAB] = (kv[:, lo:lo + HEAD_SLAB] + kr).astype(BF16)
    v_ref[0] = kv[:, hw:].astype(BF16)

    o3 = o2 + 2 * HEAD_SLAB
    bg = p[:, o3:o3 + d_conv]
    z = p[:, o3 + d_conv:o3 + 2 * d_conv] * p[:, o3 + 2 * d_conv:o3 + 3 * d_conv]
    zbuf[8:8 + tm, :] = z
    conv = (cb_ref[...] + cw_ref[0:1, :] * zbuf[6:6 + tm, :] + cw_ref[1:2, :] * zbuf[7:7 + tm, :]
            + cw_ref[2:3, :] * z)
    cn_ref[0] = _rms(bg * conv, cg_ref[...]).astype(BF16)
    tail = zbuf[tm:tm + 8, :]
    zt_ref[0] = tail
    zbuf[0:8, :] = tail


def _proj_call(x, zprev, w, cos, sin, tm):
    b, s, d = x.shape
    d_conv = w['conv_b'].shape[-1]
    hw = ATT_HEADS * HEAD_SLAB
    nt = s // tm
    q_scale = float((QK_NOPE + QK_ROPE) ** -0.5 * LOG2E)
    const = lambda bi, ti: (0, 0)
    kern = functools.partial(_proj_kernel, tm=tm, d_conv=d_conv, q_scale=q_scale)
    return pl.pallas_call(
        kern,
        name="proj",
        grid=(b, nt),
        in_specs=[
            pl.BlockSpec((1, tm, d), lambda bi, ti: (bi, ti, 0)),
            pl.BlockSpec((8, d_conv), const),
            pl.BlockSpec((1, d), const),
            pl.BlockSpec(w['w_in'].shape, const),
            pl.BlockSpec((1, Q_LORA), const),
            pl.BlockSpec(w['w_q'].shape, const),
            pl.BlockSpec((1, KV_LORA), const),
            pl.BlockSpec(w['w_kv'].shape, const),
            pl.BlockSpec((CONV_WIDTH, d_conv), const),
            pl.BlockSpec((1, d_conv), const),
            pl.BlockSpec((1, d_conv), const),
            pl.BlockSpec((tm, HEAD_SLAB), lambda bi, ti: (ti, 0)),
            pl.BlockSpec((tm, HEAD_SLAB), lambda bi, ti: (ti, 0)),
        ],
        out_specs=[
            pl.BlockSpec((1, tm, hw), lambda bi, ti: (bi, ti, 0)),
            pl.BlockSpec((1, tm, hw), lambda bi, ti: (bi, ti, 0)),
            pl.BlockSpec((1, tm, hw), lambda bi, ti: (bi, ti, 0)),
            pl.BlockSpec((1, tm, d_conv), lambda bi, ti: (bi, ti, 0)),
            pl.BlockSpec((1, 8, d_conv), lambda bi, ti: (bi, ti, 0)),
        ],
        out_shape=[
            jax.ShapeDtypeStruct((b, s, hw), BF16),
            jax.ShapeDtypeStruct((b, s, hw), BF16),
            jax.ShapeDtypeStruct((b, s, hw), BF16),
            jax.ShapeDtypeStruct((b, s, d_conv), BF16),
            jax.ShapeDtypeStruct((b, nt * 8, d_conv), F32),
        ],
        scratch_shapes=[pltpu.VMEM((tm + 8, d_conv), F32)],
        compiler_params=pltpu.CompilerParams(
            dimension_semantics=("arbitrary", "arbitrary"), vmem_limit_bytes=VMEM_LIMIT),
    )(x, zprev, w['norm1_g'], w['w_in'], w['q_norm_g'], w['w_q'], w['kv_norm_g'], w['w_kv'],
      w['conv_w'], w['conv_b'], w['conv_out_g'], cos, sin)


def _attn_kernel(q_ref, k_ref, v_ref, km_ref, vm_ref, o_ref, *, tq):
    i = pl.program_id(2)
    row = lax.broadcasted_iota(jnp.int32, (tq, tq), 0)
    col = lax.broadcasted_iota(jnp.int32, (tq, tq), 1)
    causal = col <= row
    meta_ok = lax.broadcasted_iota(jnp.int32, (tq, LANES), 1) < N_META

    def step(carry, qh, kc, vc, mask):
        m, l, acc = carry
        s = _dot_nt(qh, kc)
        if mask is not None:
            s = jnp.where(mask, s, NEG_BIG)
        m_new = jnp.maximum(m, jnp.max(s, axis=-1, keepdims=True))
        alpha = jnp.exp2(m - m_new)
        pr = jnp.exp2(s - m_new)
        l = alpha * l + jnp.sum(pr, axis=-1, keepdims=True)
        acc = alpha * acc + _dot(pr.astype(BF16), vc)
        return m_new, l, acc

    qs = [q_ref[0, :, hh * HEAD_SLAB:(hh + 1) * HEAD_SLAB] for hh in range(2)]
    init = (jnp.full((tq, 1), NEG_BIG, F32), jnp.zeros((tq, 1), F32), jnp.zeros((tq, LANES), F32))
    carries = []
    for hh in range(2):
        lo = hh * HEAD_SLAB
        carries.append(step(init, qs[hh], km_ref[:, lo:lo + HEAD_SLAB], vm_ref[:, lo:lo + HEAD_SLAB],
                            meta_ok))

    def body(j, cs):
        off = pl.multiple_of(j * tq, tq)
        out = []
        for hh in range(2):
            lo = hh * HEAD_SLAB
            out.append(step(cs[hh], qs[hh], k_ref[0, pl.ds(off, tq), lo:lo + HEAD_SLAB],
                            v_ref[0, pl.ds(off, tq), lo:lo + HEAD_SLAB], None))
        return tuple(out)

    carries = lax.fori_loop(0, i, body, tuple(carries))
    off = pl.multiple_of(i * tq, tq)
    o = None
    for hh in range(2):
        lo = hh * HEAD_SLAB
        m, l, acc = step(carries[hh], qs[hh], k_ref[0, pl.ds(off, tq), lo:lo + HEAD_SLAB],
                         v_ref[0, pl.ds(off, tq), lo:lo + HEAD_SLAB], causal)
        oh = acc / l
        o = oh if o is None else o + oh
    o_ref[0] = o.astype(BF16)


def _attn_call(q, k, v, km, vm, tq):
    b, s, hw = q.shape
    npair = ATT_HEADS // 2
    pw = 2 * HEAD_SLAB
    kern = functools.partial(_attn_kernel, tq=tq)
    return pl.pallas_call(
        kern,
        name="attn",
        grid=(b, npair, s // tq),
        in_specs=[
            pl.BlockSpec((1, tq, pw), lambda bi, pi, qi: (bi, qi, pi)),
            pl.BlockSpec((1, s, pw), lambda bi, pi, qi: (bi, 0, pi)),
            pl.BlockSpec((1, s, pw), lambda bi, pi, qi: (bi, 0, pi)),
            pl.BlockSpec((LANES, pw), lambda bi, pi, qi: (0, pi)),
            pl.BlockSpec((LANES, pw), lambda bi, pi, qi: (0, pi)),
        ],
        out_specs=pl.BlockSpec((1, tq, LANES), lambda bi, pi, qi: (bi, qi, pi)),
        out_shape=jax.ShapeDtypeStruct((b, s, D_ATT), BF16),
        compiler_params=pltpu.CompilerParams(
            dimension_semantics=("arbitrary", "arbitrary", "arbitrary"), vmem_limit_bytes=VMEM_LIMIT),
    )(q, k, v, km, vm)


def _top16_tile(s):
    key = lax.broadcasted_iota(jnp.int32, s.shape, 0)
    rank = jnp.full(s.shape, float(PEER_TOPK), F32)
    vals = []
    for r in range(PEER_TOPK):
        m = jnp.max(s, axis=0, keepdims=True)
        first = jnp.min(jnp.where(s == m, key, N_KEYS), axis=0, keepdims=True)
        hit = key == first
        rank = jnp.where(hit, float(r), rank)
        s = jnp.where(hit, -jnp.inf, s)
        vals.append(m)
    return rank, jnp.concatenate(vals, axis=0)


def _candidate_blocks():
    blocks = []
    for a in range(8):
        nb = PEER_TOPK // (a + 1)
        for b0 in range(0, nb, 8):
            nv = min(8, nb - b0)
            flat = np.array([a * PEER_TOPK + b0 + r for r in range(8)])
            blocks.append(('row', a, b0, nv, flat))
    flat = np.array([(8 + r) * PEER_TOPK for r in range(8)])
    blocks.append(('col', 8, 0, 8, flat))
    return blocks


_CAND_BLOCKS = _candidate_blocks()


def _select_counts(v1, v2):
    sub = lax.broadcasted_iota(jnp.int32, (8, LANES), 0)
    vals = []
    for kind, a0, b0, nv, _ in _CAND_BLOCKS:
        if kind == 'row':
            blk = v1[a0:a0 + 1, :] + v2[b0:b0 + 8, :]
        else:
            blk = v1[a0:a0 + 8, :] + v2[b0:b0 + 1, :]
        if nv < 8:
            blk = jnp.where(sub < nv, blk, -jnp.inf)
        vals.append(blk)
    beaten = [jnp.zeros((8, LANES), F32) for _ in vals]
    for si, (_, _, _, nv_s, flat_s) in enumerate(_CAND_BLOCKS):
        for r in range(nv_s):
            src = jnp.broadcast_to(vals[si][r:r + 1, :], (8, LANES))
            for ti, (_, _, _, _, flat_t) in enumerate(_CAND_BLOCKS):
                wins_tie = flat_s[r] < flat_t
                gt = jnp.where(src > vals[ti], 1.0, 0.0)
                ge = jnp.where(src >= vals[ti], 1.0, 0.0)
                if wins_tie.all():
                    beats = ge
                elif not wins_tie.any():
                    beats = gt
                else:
                    beats = jnp.where(sub >= int((~wins_tie).sum()), ge, gt)
                beaten[ti] = beaten[ti] + beats
    top = vals[0][0:1, :]
    z = jnp.zeros((1, LANES), F32)
    cnt_rows = [jnp.zeros((1, LANES), F32) for _ in range(PEER_TOPK)]
    for ti, (kind, a0, b0, nv, _) in enumerate(_CAND_BLOCKS):
        self_f = jnp.where(beaten[ti] < float(PEER_TOPK), 1.0, 0.0)
        if nv < 8:
            self_f = jnp.where(sub < nv, self_f, 0.0)
        z = z + jnp.sum(self_f * jnp.exp(vals[ti] - top), axis=0, keepdims=True)
        if kind == 'row':
            cnt_rows[a0] = cnt_rows[a0] + jnp.sum(self_f, axis=0, keepdims=True)
        else:
            for r in range(8):
                cnt_rows[a0 + r] = cnt_rows[a0 + r] + self_f[r:r + 1, :]
    return jnp.concatenate(cnt_rows, axis=0), z


def _mix_kernel(att_ref, cn_ref, x_ref, ag_ref, wo_ref, g2_ref, wqr_ref, k1_ref, k2_ref,
                h1_ref, xnt_ref, c_ref, e1_ref, r2_ref, e2_ref, st, rk, vs, *, tm):
    nlt = tm // LANES
    att = _rms(att_ref[...].astype(F32), ag_ref[...]).astype(BF16)
    d_att = att.shape[-1]
    h1 = x_ref[...] + _dot(att, wo_ref[0:d_att, :]) + _dot(cn_ref[...], wo_ref[d_att:, :])
    h1_ref[...] = h1
    xn32 = _rms(h1, g2_ref[...])
    xn = xn32.astype(BF16)
    xnt_ref[...] = xn32.T.astype(BF16)
    qp = _dot(xn, wqr_ref[...]).astype(BF16)
    half = D_KEY // 2
    for h in range(PEER_HEADS):
        for side, kref in enumerate((k1_ref, k2_ref)):
            lo = h * D_KEY + side * half
            sc = _dot_nt(kref[...], qp[:, lo:lo + half])
            for lt in range(nlt):
                st[2 * h + side, lt] = sc[:, lt * LANES:(lt + 1) * LANES]

    def rank_body(idx, carry):
        p = idx // nlt
        lt = idx % nlt
        rank, vals = _top16_tile(st[p, lt])
        rk[p, lt] = rank
        vs[p, lt] = vals
        return carry

    lax.fori_loop(0, 2 * PEER_HEADS * nlt, rank_body, 0)

    def sel_body(idx, carry):
        h = idx // (nlt // 2)
        lt2 = idx % (nlt // 2)
        for half in range(2):
            lt = 2 * lt2 + half
            ls = slice(half * LANES, (half + 1) * LANES)
            v1 = vs[2 * h, lt]
            v2 = vs[2 * h + 1, lt]
            cnt, z = _select_counts(v1, v2)
            r1 = rk[2 * h, lt]
            c = jnp.zeros((N_KEYS, LANES), F32)
            for a in range(PEER_TOPK):
                c = c + jnp.where(r1 == float(a), cnt[a:a + 1, :], 0.0)
            c_ref[h, lt] = c
            e1_ref[h, lt] = jnp.exp(st[2 * h, lt] - v1[0:1, :]) * (1.0 / z)
            r2_ref[h, lt2, :, ls] = rk[2 * h + 1, lt].astype(BF16)
            e2_ref[h, lt2, :, ls] = jnp.exp(st[2 * h + 1, lt] - v2[0:1, :]).astype(BF16)
        return carry

    lax.fori_loop(0, PEER_HEADS * (nlt // 2), sel_body, 0)


def _mix_call(att, cn, x2, w, tm):
    n, d = x2.shape
    nlt = tm // LANES
    d_mix = w['w_out'].shape[0]
    const = lambda ti: (0, 0)
    tile4 = pl.BlockSpec((PEER_HEADS, nlt, N_KEYS, LANES), lambda ti: (0, ti, 0, 0))
    tile4p = pl.BlockSpec((PEER_HEADS, nlt // 2, N_KEYS, 2 * LANES), lambda ti: (0, ti, 0, 0))
    sel_shape = (PEER_HEADS, n // LANES, N_KEYS, LANES)
    selp_shape = (PEER_HEADS, n // (2 * LANES), N_KEYS, 2 * LANES)
    kern = functools.partial(_mix_kernel, tm=tm)
    return pl.pallas_call(
        kern,
        name="mix",
        grid=(n // tm,),
        in_specs=[
            pl.BlockSpec((tm, att.shape[-1]), lambda ti: (ti, 0)),
            pl.BlockSpec((tm, cn.shape[-1]), lambda ti: (ti, 0)),
            pl.BlockSpec((tm, d), lambda ti: (ti, 0)),
            pl.BlockSpec((1, att.shape[-1]), const),
            pl.BlockSpec((d_mix, d), const),
            pl.BlockSpec((1, d), const),
            pl.BlockSpec(w['w_query'].shape, const),
            pl.BlockSpec((N_KEYS, D_KEY // 2), const),
            pl.BlockSpec((N_KEYS, D_KEY // 2), const),
        ],
        out_specs=[
            pl.BlockSpec((tm, d), lambda ti: (ti, 0)),
            pl.BlockSpec((d, tm), lambda ti: (0, ti)),
            tile4, tile4, tile4p, tile4p,
        ],
        out_shape=[
            jax.ShapeDtypeStruct((n, d), F32),
            jax.ShapeDtypeStruct((d, n), BF16),
            jax.ShapeDtypeStruct(sel_shape, F32),
            jax.ShapeDtypeStruct(sel_shape, F32),
            jax.ShapeDtypeStruct(selp_shape, BF16),
            jax.ShapeDtypeStruct(selp_shape, BF16),
        ],
        scratch_shapes=[
            pltpu.VMEM((2 * PEER_HEADS, nlt, N_KEYS, LANES), F32),
            pltpu.VMEM((2 * PEER_HEADS, nlt, N_KEYS, LANES), F32),
            pltpu.VMEM((2 * PEER_HEADS, nlt, PEER_TOPK, LANES), F32),
        ],
        compiler_params=pltpu.CompilerParams(
            dimension_semantics=("arbitrary",), vmem_limit_bytes=VMEM_LIMIT),
    )(att, cn, x2, w['attn_out_g'], w['w_out'], w['norm2_g'], w['w_query'], w['keys1'], w['keys2'])


def _gelu_tanh(x):
    c = float(np.sqrt(2.0 / np.pi))
    hx = 0.5 * x
    return hx + hx * jnp.tanh(x * (c + (c * 0.044715) * (x * x)))


PACK_ROWS = 16


def _peer_kernel(xnt_ref, u_ref, vt_ref, c_ref, e1_ref, r2_ref, e2_ref, h1_ref, fg_ref, y_ref,
                 acc, pbuf, tbl, *, t, ec):
    e = pl.program_id(1)
    nlt = t // LANES
    nib = ec // N_KEYS
    nsub = ec // SUB_EXPERTS
    kps = SUB_EXPERTS // N_KEYS
    npos = N_KEYS // PACK_ROWS

    @pl.when(e == 0)
    def _():
        acc[...] = jnp.zeros_like(acc)

    row0 = pl.multiple_of(e * nib, nib)
    for h in range(PEER_HEADS):
        for lt in range(nlt):
            ls = slice((lt % 2) * LANES, (lt % 2 + 1) * LANES)
            c8 = c_ref[h, lt, pl.ds(row0, nib), :]
            e8 = e1_ref[h, lt, pl.ds(row0, nib), :]
            for kk in range(nib):
                tbl[0, h, lt // 2, kk, :, ls] = jnp.broadcast_to(c8[kk:kk + 1, :], (PACK_ROWS, LANES)).astype(BF16)
                tbl[1, h, lt // 2, kk, :, ls] = jnp.broadcast_to(e8[kk:kk + 1, :], (PACK_ROWS, LANES)).astype(BF16)

    for sb in range(nsub):
        act = _dot(u_ref[sb * SUB_EXPERTS:(sb + 1) * SUB_EXPERTS, :], xnt_ref[...])
        for k in range(kps):
            for lt2 in range(nlt // 2):
                ls = slice(lt2 * 2 * LANES, (lt2 + 1) * 2 * LANES)
                gate = None
                for h in range(PEER_HEADS):
                    cb = pltpu.repeat(tbl[0, h, lt2, sb * kps + k], npos, axis=0)
                    eb = pltpu.repeat(tbl[1, h, lt2, sb * kps + k], npos, axis=0)
                    term = jnp.where(r2_ref[h, lt2] < cb, e2_ref[h, lt2] * eb, jnp.zeros((), BF16))
                    gate = term if gate is None else gate + term
                gl = _gelu_tanh(act[k * N_KEYS:(k + 1) * N_KEYS, ls]).astype(BF16)
                pbuf[sb, k * N_KEYS:(k + 1) * N_KEYS, ls] = gate * gl
        acc[...] += _dot(vt_ref[sb], pbuf[sb])

    @pl.when(e == pl.num_programs(1) - 1)
    def _():
        h2 = h1_ref[...] + acc[...].T
        y_ref[...] = _rms(h2, fg_ref[...])


def _peer_call(xnt, u, vt, c, e1, r2, e2, h1, fg, t, ec):
    d, n = xnt.shape
    ne = u.shape[0] // ec
    nsub = ec // SUB_EXPERTS
    assert vt.shape == (ne * nsub, d, SUB_EXPERTS)
    nlt = t // LANES
    tile4 = pl.BlockSpec((PEER_HEADS, nlt, N_KEYS, LANES), lambda ti, ei: (0, ti, 0, 0))
    tile4p = pl.BlockSpec((PEER_HEADS, nlt // 2, N_KEYS, 2 * LANES), lambda ti, ei: (0, ti, 0, 0))
    kern = functools.partial(_peer_kernel, t=t, ec=ec)
    return pl.pallas_call(
        kern,
        name="peer",
        grid=(n // t, ne),
        in_specs=[
            pl.BlockSpec((d, t), lambda ti, ei: (0, ti)),
            pl.BlockSpec((ec, d), lambda ti, ei: (ei, 0)),
            pl.BlockSpec((nsub, d, SUB_EXPERTS), lambda ti, ei: (ei, 0, 0)),
            tile4, tile4, tile4p, tile4p,
            pl.BlockSpec((t, d), lambda ti, ei: (ti, 0)),
            pl.BlockSpec((1, d), lambda ti, ei: (0, 0)),
        ],
        out_specs=pl.BlockSpec((t, d), lambda ti, ei: (ti, 0)),
        out_shape=jax.ShapeDtypeStruct((n, d), F32),
        scratch_shapes=[pltpu.VMEM((d, t), F32), pltpu.VMEM((nsub, SUB_EXPERTS, t), BF16),
                        pltpu.VMEM((2, PEER_HEADS, nlt // 2, ec // N_KEYS, PACK_ROWS, 2 * LANES), BF16)],
        compiler_params=pltpu.CompilerParams(
            dimension_semantics=("arbitrary", "arbitrary"), vmem_limit_bytes=VMEM_LIMIT),
    )(xnt, u, vt, c, e1, r2, e2, h1, fg)


def _rope_slab_tables(length):
    inv = 1.0 / (ROPE_THETA ** (jnp.arange(0, QK_ROPE, 2, dtype=F32) / QK_ROPE))
    ang = jnp.arange(length, dtype=F32)[:, None] * inv[None, :]
    cos, sin = jnp.cos(ang), jnp.sin(ang)
    half = QK_ROPE // 2
    pad = HEAD_SLAB - QK_NOPE - QK_ROPE
    ones = jnp.ones((length, QK_NOPE), F32)
    zeros_n = jnp.zeros((length, QK_NOPE), F32)
    zeros_p = jnp.zeros((length, pad), F32)
    del half
    cos_t = jnp.concatenate([ones, cos, cos, zeros_p], axis=1)
    sin_t = jnp.concatenate([zeros_n, -sin, sin, zeros_p], axis=1)
    return cos_t, sin_t


def _head_slabs(wcols, n_heads, width, offset=0):
    k = wcols.shape[0]
    w3 = wcols.reshape(k, n_heads, width)
    out = jnp.zeros((k, n_heads, HEAD_SLAB), wcols.dtype)
    if isinstance(offset, int):
        return out.at[:, :, offset:offset + width].set(w3).reshape(k, n_heads * HEAD_SLAB)
    for h in range(n_heads):
        out = out.at[:, h, offset[h]:offset[h] + width].set(w3[:, h])
    return out.reshape(k, n_heads * HEAD_SLAB)


def _prep_weights(norm1_g, w_in, q_norm_g, w_uq, kv_norm_g, w_ukv, conv_w, conv_b, attn_out_g,
                  conv_out_g, w_out, norm2_g, peer_w_query, peer_keys1, peer_keys2):
    half = QK_ROPE // 2
    o1 = Q_LORA
    o2 = o1 + KV_LORA
    o3 = o2 + QK_ROPE
    k_in = w_in.shape[0]
    kr = w_in[:, o2:o3]
    zpad_n = jnp.zeros((k_in, QK_NOPE), F32)
    zpad_p = jnp.zeros((k_in, HEAD_SLAB - QK_NOPE - QK_ROPE), F32)
    kr_a = jnp.concatenate([zpad_n, kr[:, :half], kr[:, half:], zpad_p], axis=1)
    kr_b = jnp.concatenate([zpad_n, kr[:, half:], kr[:, :half], zpad_p], axis=1)
    w_in_ext = jnp.concatenate([w_in[:, :o2], kr_a, kr_b, w_in[:, o3:]], axis=1).astype(BF16)

    qd = QK_NOPE + QK_ROPE
    wq3 = w_uq.reshape(Q_LORA, ATT_HEADS, qd)
    wq_swap = jnp.concatenate([wq3[:, :, :QK_NOPE], wq3[:, :, QK_NOPE + half:], wq3[:, :, QK_NOPE:QK_NOPE + half]],
                              axis=2).reshape(Q_LORA, ATT_HEADS * qd)
    w_q = jnp.concatenate([_head_slabs(w_uq, ATT_HEADS, qd), _head_slabs(wq_swap, ATT_HEADS, qd)],
                          axis=1).astype(BF16)

    wkv3 = w_ukv.reshape(KV_LORA, ATT_HEADS, QK_NOPE + V_HEAD)
    wk = wkv3[:, :, :QK_NOPE].reshape(KV_LORA, ATT_HEADS * QK_NOPE)
    wv = wkv3[:, :, QK_NOPE:].reshape(KV_LORA, ATT_HEADS * V_HEAD)
    v_off = [V_HEAD * (h % 2) for h in range(ATT_HEADS)]
    w_kv = jnp.concatenate([_head_slabs(wk, ATT_HEADS, QK_NOPE), _head_slabs(wv, ATT_HEADS, V_HEAD, v_off)],
                           axis=1).astype(BF16)
    return {
        'norm1_g': norm1_g.reshape(1, -1), 'w_in': w_in_ext,
        'q_norm_g': q_norm_g.reshape(1, -1), 'w_q': w_q,
        'kv_norm_g': kv_norm_g.reshape(1, -1), 'w_kv': w_kv,
        'conv_w': conv_w, 'conv_b': conv_b.reshape(1, -1), 'conv_out_g': conv_out_g.reshape(1, -1),
        'attn_out_g': attn_out_g.reshape(1, -1), 'w_out': w_out.astype(BF16),
        'norm2_g': norm2_g.reshape(1, -1), 'w_query': peer_w_query.astype(BF16),
        'keys1': peer_keys1.astype(BF16), 'keys2': peer_keys2.astype(BF16),
    }


def kernel(x, meta_tokens, norm1_g, w_in, q_norm_g, w_uq, kv_norm_g, w_ukv, conv_w, conv_b, attn_out_g,
           conv_out_g, w_out, norm2_g, peer_w_query, peer_keys1, peer_keys2, peer_u, peer_v, final_g):
    b, s, d = x.shape
    assert norm1_g.shape[0] == 1, "single-layer kernel"
    assert meta_tokens.shape[0] == N_META
    tm = min(TM_PROJ, s)
    assert s % tm == 0 and s % LANES == 0
    w = _prep_weights(norm1_g[0], w_in[0], q_norm_g[0], w_uq[0], kv_norm_g[0], w_ukv[0], conv_w[0],
                      conv_b[0], attn_out_g[0], conv_out_g[0], w_out[0], norm2_g[0], peer_w_query[0],
                      peer_keys1[0], peer_keys2[0])
    d_conv = conv_b.shape[-1]
    cos_t, sin_t = _rope_slab_tables(N_META + s)

    _, k_m, v_m, _, z_m = _proj_call(meta_tokens[None], jnp.zeros((8, d_conv), F32), w,
                                     cos_t[:N_META], sin_t[:N_META], N_META)
    q, k, v, cn, _ = _proj_call(x, z_m[0], w, cos_t[N_META:], sin_t[N_META:], tm)
    hw = ATT_HEADS * HEAD_SLAB
    km = jnp.zeros((LANES, hw), BF16).at[:N_META].set(k_m[0])
    vm = jnp.zeros((LANES, hw), BF16).at[:N_META].set(v_m[0])
    att = _attn_call(q, k, v, km, vm, min(TQ_ATT, s))

    n = b * s
    tmix = min(TM_MIX, n)
    h1, xnt, c, e1, r2, e2 = _mix_call(att.reshape(n, D_ATT), cn.reshape(n, d_conv), x.reshape(n, d), w, tmix)
    u_bf = peer_u[0].astype(BF16)
    vt_bf = peer_v[0].reshape(-1, SUB_EXPERTS, d).transpose(0, 2, 1).astype(BF16)
    y = _peer_call(xnt, u_bf, vt_bf, c, e1, r2, e2, h1, final_g.reshape(1, -1), min(T_PEER, n), E_CHUNK)
    return y.reshape(b, s, d)
```

```python
import functools

import numpy as np
import jax
import jax.numpy as jnp
from jax import lax
from jax.experimental import pallas as pl
from jax.experimental.pallas import tpu as pltpu

F32 = jnp.float32
BF16 = jnp.bfloat16

N_META = 16
EPS = 1e-6
ROPE_THETA = 10000.0
ATT_HEADS = 8
Q_LORA = 256
KV_LORA = 128
QK_NOPE = 64
QK_ROPE = 32
V_HEAD = 64
D_ATT = ATT_HEADS * V_HEAD
PEER_HEADS = 8
N_KEYS = 128
D_KEY = 256
PEER_TOPK = 16
CONV_WIDTH = 3

LANES = 128
HEAD_SLAB = LANES
NEG_BIG = -1e30
LOG2E = 1.4426950408889634
VMEM_LIMIT = 56 * 1024 * 1024

TM_PROJ = 512
TQ_ATT = 512
TK_ATT = 512
HEADS_PER_STEP = 4
TM_MIX = 512
T_PEER = 512
E_CHUNK = 1024
SUB_EXPERTS = 512


def _dot(a, b):
    return jnp.dot(a, b, preferred_element_type=F32)


def _dot_nt(a, b):
    return lax.dot_general(a, b, (((1,), (1,)), ((), ())), preferred_element_type=F32)


def _rms(x, g):
    y = x * lax.rsqrt(jnp.mean(x * x, axis=-1, keepdims=True) + EPS)
    return y * g


def _proj_kernel(x_ref, zprev_ref, g1_ref, win_ref, qg_ref, wq_ref, kvg_ref, wkv_ref, cw_ref, cb_ref,
                 cg_ref, cos_ref, sin_ref,
                 q_ref, k_ref, v_ref, cn_ref, zt_ref, zbuf, *, tm, d_conv, q_scale):
    t = pl.program_id(1)

    @pl.when(t == 0)
    def _():
        zbuf[0:8, :] = zprev_ref[...]

    x = x_ref[0]
    hn = _rms(x, g1_ref[...]).astype(BF16)
    p = _dot(hn, win_ref[...])
    cos = cos_ref[...]
    sin = sin_ref[...]

    qn = _rms(p[:, :Q_LORA], qg_ref[...]).astype(BF16)
    qq = _dot(qn, wq_ref[...])
    hw = ATT_HEADS * HEAD_SLAB
    for h in range(ATT_HEADS):
        lo = h * HEAD_SLAB
        qh = qq[:, lo:lo + HEAD_SLAB] * cos + qq[:, hw + lo:hw + lo + HEAD_SLAB] * sin
        q_ref[0, :, lo:lo + HEAD_SLAB] = (qh * q_scale).astype(BF16)

    o1 = Q_LORA
    o2 = o1 + KV_LORA
    kvn = _rms(p[:, o1:o2], kvg_ref[...]).astype(BF16)
    kv = _dot(kvn, wkv_ref[...])
    kr = p[:, o2:o2 + HEAD_SLAB] * cos + p[:, o2 + HEAD_SLAB:o2 + 2 * HEAD_SLAB] * sin
    for h in range(ATT_HEADS):
        lo = h * HEAD_SLAB
        k_ref[0, :, lo:lo + HEAD_SLAB] = (kv[:, lo:lo + HEAD_SLAB] + kr).astype(BF16)
    v_ref[0] = kv[:, hw:].astype(BF16)

    o3 = o2 + 2 * HEAD_SLAB
    bg = p[:, o3:o3 + d_conv]
    z = p[:, o3 + d_conv:o3 + 2 * d_conv] * p[:, o3 + 2 * d_conv:o3 + 3 * d_conv]
    zbuf[8:8 + tm, :] = z
    conv = (cb_ref[...] + cw_ref[0:1, :] * zbuf[6:6 + tm, :] + cw_ref[1:2, :] * zbuf[7:7 + tm, :]
            + cw_ref[2:3, :] * z)
    cn_ref[0] = _rms(bg * conv, cg_ref[...]).astype(BF16)
    tail = zbuf[tm:tm + 8, :]
    zt_ref[0] = tail
    zbuf[0:8, :] = tail


def _proj_call(x, zprev, w, cos, sin, tm):
    b, s, d = x.shape
    d_conv = w['conv_b'].shape[-1]
    hw = ATT_HEADS * HEAD_SLAB
    nt = s // tm
    q_scale = float((QK_NOPE + QK_ROPE) ** -0.5 * LOG2E)
    const = lambda bi, ti: (0, 0)
    kern = functools.partial(_proj_kernel, tm=tm, d_conv=d_conv, q_scale=q_scale)
    return pl.pallas_call(
        kern,
        name="proj",
        grid=(b, nt),
        in_specs=[
            pl.BlockSpec((1, tm, d), lambda bi, ti: (bi, ti, 0)),
            pl.BlockSpec((8, d_conv), const),
            pl.BlockSpec((1, d), const),
            pl.BlockSpec(w['w_in'].shape, const),
            pl.BlockSpec((1, Q_LORA), const),
            pl.BlockSpec(w['w_q'].shape, const),
            pl.BlockSpec((1, KV_LORA), const),
            pl.BlockSpec(w['w_kv'].shape, const),
            pl.BlockSpec((CONV_WIDTH, d_conv), const),
            pl.BlockSpec((1, d_conv), const),
            pl.BlockSpec((1, d_conv), const),
            pl.BlockSpec((tm, HEAD_SLAB), lambda bi, ti: (ti, 0)),
            pl.BlockSpec((tm, HEAD_SLAB), lambda bi, ti: (ti, 0)),
        ],
        out_specs=[
            pl.BlockSpec((1, tm, hw), lambda bi, ti: (bi, ti, 0)),
            pl.BlockSpec((1, tm, hw), lambda bi, ti: (bi, ti, 0)),
            pl.BlockSpec((1, tm, hw), lambda bi, ti: (bi, ti, 0)),
            pl.BlockSpec((1, tm, d_conv), lambda bi, ti: (bi, ti, 0)),
            pl.BlockSpec((1, 8, d_conv), lambda bi, ti: (bi, ti, 0)),
        ],
        out_shape=[
            jax.ShapeDtypeStruct((b, s, hw), BF16),
            jax.ShapeDtypeStruct((b, s, hw), BF16),
            jax.ShapeDtypeStruct((b, s, hw), BF16),
            jax.ShapeDtypeStruct((b, s, d_conv), BF16),
            jax.ShapeDtypeStruct((b, nt * 8, d_conv), F32),
        ],
        scratch_shapes=[pltpu.VMEM((tm + 8, d_conv), F32)],
        compiler_params=pltpu.CompilerParams(
            dimension_semantics=("arbitrary", "arbitrary"), vmem_limit_bytes=VMEM_LIMIT),
    )(x, zprev, w['norm1_g'], w['w_in'], w['q_norm_g'], w['w_q'], w['kv_norm_g'], w['w_kv'],
      w['conv_w'], w['conv_b'], w['conv_out_g'], cos, sin)


def _attn_kernel(q_ref, k_ref, v_ref, km_ref, vm_ref, o_ref, *, tq, tk, hg):
    i = pl.program_id(2)
    nd = tq // tk
    row = lax.broadcasted_iota(jnp.int32, (tq, tk), 0)
    col = lax.broadcasted_iota(jnp.int32, (tq, tk), 1)
    meta_ok = lax.broadcasted_iota(jnp.int32, (tq, LANES), 1) < N_META

    def step(carry, qh, kc, vc, mask):
        m, l, acc = carry
        s = _dot_nt(qh, kc)
        if mask is not None:
            s = jnp.where(mask, s, NEG_BIG)
        m_new = jnp.maximum(m, jnp.max(s, axis=-1, keepdims=True))
        alpha = jnp.exp2(m - m_new)
        pr = jnp.exp2(s - m_new)
        l = alpha * l + jnp.sum(pr, axis=-1, keepdims=True)
        acc = alpha * acc + _dot(pr.astype(BF16), vc)
        return m_new, l, acc

    slabs = [slice(hh * HEAD_SLAB, (hh + 1) * HEAD_SLAB) for hh in range(hg)]
    qs = [q_ref[0, :, sl] for sl in slabs]
    init = (jnp.full((tq, 1), NEG_BIG, F32), jnp.zeros((tq, 1), F32), jnp.zeros((tq, LANES), F32))
    carries = tuple(step(init, qs[hh], km_ref[:, slabs[hh]], vm_ref[:, slabs[hh]], meta_ok)
                    for hh in range(hg))

    def body(j, cs):
        rows = pl.ds(pl.multiple_of(j * tk, tk), tk)
        return tuple(step(cs[hh], qs[hh], k_ref[0, rows, slabs[hh]], v_ref[0, rows, slabs[hh]], None)
                     for hh in range(hg))

    carries = lax.fori_loop(0, i * nd, body, carries)
    for c in range(nd):
        rows = pl.ds(pl.multiple_of(i * tq + c * tk, tk), tk)
        causal = col + c * tk <= row
        carries = tuple(step(carries[hh], qs[hh], k_ref[0, rows, slabs[hh]], v_ref[0, rows, slabs[hh]],
                             causal) for hh in range(hg))
    outs = [acc / l for _, l, acc in carries]
    o_ref[0] = jnp.concatenate([outs[2 * p] + outs[2 * p + 1] for p in range(hg // 2)], axis=-1).astype(BF16)


def _attn_call(q, k, v, km, vm, tq, hg):
    b, s, hw = q.shape
    gw = hg * HEAD_SLAB
    ow = (hg // 2) * LANES
    tk = min(TK_ATT, tq)
    kern = functools.partial(_attn_kernel, tq=tq, tk=tk, hg=hg)
    return pl.pallas_call(
        kern,
        name="attn",
        grid=(b, ATT_HEADS // hg, s // tq),
        in_specs=[
            pl.BlockSpec((1, tq, gw), lambda bi, gi, qi: (bi, qi, gi)),
            pl.BlockSpec((1, s, gw), lambda bi, gi, qi: (bi, 0, gi)),
            pl.BlockSpec((1, s, gw), lambda bi, gi, qi: (bi, 0, gi)),
            pl.BlockSpec((LANES, gw), lambda bi, gi, qi: (0, gi)),
            pl.BlockSpec((LANES, gw), lambda bi, gi, qi: (0, gi)),
        ],
        out_specs=pl.BlockSpec((1, tq, ow), lambda bi, gi, qi: (bi, qi, gi)),
        out_shape=jax.ShapeDtypeStruct((b, s, D_ATT), BF16),
        compiler_params=pltpu.CompilerParams(
            dimension_semantics=("arbitrary", "arbitrary", "arbitrary"), vmem_limit_bytes=VMEM_LIMIT),
    )(q, k, v, km, vm)


def _rank16_by_extraction(s):
    key = lax.broadcasted_iota(jnp.int32, s.shape, 0)
    rank = jnp.full(s.shape, float(PEER_TOPK), F32)
    for r in range(PEER_TOPK):
        m = jnp.max(s, axis=0, keepdims=True)
        first = jnp.min(jnp.where(s == m, key, N_KEYS), axis=0, keepdims=True)
        hit = key == first
        rank = jnp.where(hit, float(r), rank)
        s = jnp.where(hit, -jnp.inf, s)
    return rank


def _oddeven_merge(lo, hi, r):
    step = r * 2
    if step < hi - lo:
        yield from _oddeven_merge(lo, hi, step)
        yield from _oddeven_merge(lo + r, hi, step)
        yield from [(i, i + r) for i in range(lo + r, hi - r, step)]
    else:
        yield (lo, lo + r)


def _oddeven_merge_sort(lo, hi):
    if hi - lo >= 1:
        mid = lo + (hi - lo) // 2
        yield from _oddeven_merge_sort(lo, mid)
        yield from _oddeven_merge_sort(mid + 1, hi)
        yield from _oddeven_merge(lo, hi, 1)


_SORT16_NET = tuple(_oddeven_merge_sort(0, PEER_TOPK - 1))
SUBLANES = 8


def _sorted_top16(s):
    cols = [s[SUBLANES * v:SUBLANES * (v + 1), :] for v in range(N_KEYS // SUBLANES)]

    def exchange(i, j):
        hi, lo = jnp.maximum(cols[i], cols[j]), jnp.minimum(cols[i], cols[j])
        cols[i], cols[j] = hi, lo

    for i, j in _SORT16_NET:
        exchange(i, j)
    for shift in (4, 2, 1):
        partner = [pltpu.roll(cols[PEER_TOPK - 1 - r], shift, axis=0) for r in range(PEER_TOPK)]
        cols = [jnp.maximum(cols[r], partner[r]) for r in range(PEER_TOPK)]
        for stride in (8, 4, 2, 1):
            for r in range(PEER_TOPK):
                if r & stride == 0:
                    exchange(r, r + stride)
    return cols


def _top16_tile(s):
    top = _sorted_top16(s)
    vals = jnp.concatenate([t[0:1, :] for t in top], axis=0)
    nv = N_KEYS // SUBLANES
    rows = [s[SUBLANES * v:SUBLANES * (v + 1), :] for v in range(nv)]
    at_least = jnp.zeros((SUBLANES, LANES), F32)
    for v in range(nv):
        at_least = at_least + jnp.where(rows[v] >= top[PEER_TOPK - 1], 1.0, 0.0)
    tied = jnp.where(jnp.sum(at_least, axis=0, keepdims=True) != float(PEER_TOPK), 1.0, 0.0)
    for r in range(PEER_TOPK - 1):
        tied = tied + jnp.where(top[r][0:1, :] <= top[r + 1][0:1, :], 1.0, 0.0)

    def by_count(x):
        out = []
        for v in range(nv):
            rv = jnp.zeros((SUBLANES, LANES), F32)
            for r in range(PEER_TOPK):
                rv = rv + jnp.where(top[r] > x[SUBLANES * v:SUBLANES * (v + 1), :], 1.0, 0.0)
            out.append(rv)
        return jnp.concatenate(out, axis=0)

    rank = lax.cond(jnp.max(tied) > 0.0, _rank16_by_extraction, by_count, s)
    return rank, vals


def _candidate_blocks():
    blocks = []
    for a in range(8):
        nb = PEER_TOPK // (a + 1)
        for b0 in range(0, nb, 8):
            nv = min(8, nb - b0)
            flat = np.array([a * PEER_TOPK + b0 + r for r in range(8)])
            blocks.append(('row', a, b0, nv, flat))
    flat = np.array([(8 + r) * PEER_TOPK for r in range(8)])
    blocks.append(('col', 8, 0, 8, flat))
    return blocks


_CAND_BLOCKS = _candidate_blocks()


def _select_counts(v1, v2):
    sub = lax.broadcasted_iota(jnp.int32, (8, LANES), 0)
    vals = []
    for kind, a0, b0, nv, _ in _CAND_BLOCKS:
        if kind == 'row':
            blk = v1[a0:a0 + 1, :] + v2[b0:b0 + 8, :]
        else:
            blk = v1[a0:a0 + 8, :] + v2[b0:b0 + 1, :]
        if nv < 8:
            blk = jnp.where(sub < nv, blk, -jnp.inf)
        vals.append(blk)
    beaten = [jnp.zeros((8, LANES), F32) for _ in vals]
    for si, (_, _, _, nv_s, flat_s) in enumerate(_CAND_BLOCKS):
        for r in range(nv_s):
            src = jnp.broadcast_to(vals[si][r:r + 1, :], (8, LANES))
            for ti, (_, _, _, _, flat_t) in enumerate(_CAND_BLOCKS):
                wins_tie = flat_s[r] < flat_t
                gt = jnp.where(src > vals[ti], 1.0, 0.0)
                ge = jnp.where(src >= vals[ti], 1.0, 0.0)
                if wins_tie.all():
                    beats = ge
                elif not wins_tie.any():
                    beats = gt
                else:
                    beats = jnp.where(sub >= int((~wins_tie).sum()), ge, gt)
                beaten[ti] = beaten[ti] + beats
    top = vals[0][0:1, :]
    z = jnp.zeros((1, LANES), F32)
    cnt_rows = [jnp.zeros((1, LANES), F32) for _ in range(PEER_TOPK)]
    for ti, (kind, a0, b0, nv, _) in enumerate(_CAND_BLOCKS):
        self_f = jnp.where(beaten[ti] < float(PEER_TOPK), 1.0, 0.0)
        if nv < 8:
            self_f = jnp.where(sub < nv, self_f, 0.0)
        z = z + jnp.sum(self_f * jnp.exp(vals[ti] - top), axis=0, keepdims=True)
        if kind == 'row':
            cnt_rows[a0] = cnt_rows[a0] + jnp.sum(self_f, axis=0, keepdims=True)
        else:
            for r in range(8):
                cnt_rows[a0 + r] = cnt_rows[a0 + r] + self_f[r:r + 1, :]
    return jnp.concatenate(cnt_rows, axis=0), z


def _mix_kernel(att_ref, cn_ref, x_ref, ag_ref, wo_ref, g2_ref, wqr_ref, k1_ref, k2_ref,
                h1_ref, xnt_ref, c_ref, e1_ref, r2_ref, e2_ref, st, rk, vs, *, tm):
    nlt = tm // LANES
    att = _rms(att_ref[...].astype(F32), ag_ref[...]).astype(BF16)
    d_att = att.shape[-1]
    h1 = x_ref[...] + _dot(att, wo_ref[0:d_att, :]) + _dot(cn_ref[...], wo_ref[d_att:, :])
    h1_ref[...] = h1
    xn32 = _rms(h1, g2_ref[...])
    xn = xn32.astype(BF16)
    xnt_ref[...] = xn32.T.astype(BF16)
    qp = _dot(xn, wqr_ref[...]).astype(BF16)
    half = D_KEY // 2
    for h in range(PEER_HEADS):
        for side, kref in enumerate((k1_ref, k2_ref)):
            lo = h * D_KEY + side * half
            sc = _dot_nt(kref[...], qp[:, lo:lo + half])
            for lt in range(nlt):
                st[2 * h + side, lt] = sc[:, lt * LANES:(lt + 1) * LANES]

    def rank_body(idx, carry):
        p = idx // nlt
        lt = idx % nlt
        rank, vals = _top16_tile(st[p, lt])
        rk[p, lt] = rank
        vs[p, lt] = vals
        return carry

    lax.fori_loop(0, 2 * PEER_HEADS * nlt, rank_body, 0)

    def sel_body(idx, carry):
        h = idx // (nlt // 2)
        lt2 = idx % (nlt // 2)
        for half in range(2):
            lt = 2 * lt2 + half
            ls = slice(half * LANES, (half + 1) * LANES)
            v1 = vs[2 * h, lt]
            v2 = vs[2 * h + 1, lt]
            cnt, z = _select_counts(v1, v2)
            r1 = rk[2 * h, lt]
            c = jnp.zeros((N_KEYS, LANES), F32)
            for a in range(PEER_TOPK):
                c = c + jnp.where(r1 == float(a), cnt[a:a + 1, :], 0.0)
            c_ref[h, lt] = c
            e1_ref[h, lt] = jnp.exp(st[2 * h, lt] - v1[0:1, :]) * (1.0 / z)
            r2_ref[h, lt2, :, ls] = rk[2 * h + 1, lt].astype(BF16)
            e2_ref[h, lt2, :, ls] = jnp.exp(st[2 * h + 1, lt] - v2[0:1, :]).astype(BF16)
        return carry

    lax.fori_loop(0, PEER_HEADS * (nlt // 2), sel_body, 0)


def _mix_call(att, cn, x2, w, tm):
    n, d = x2.shape
    nlt = tm // LANES
    d_mix = w['w_out'].shape[0]
    const = lambda ti: (0, 0)
    tile4 = pl.BlockSpec((PEER_HEADS, nlt, N_KEYS, LANES), lambda ti: (0, ti, 0, 0))
    tile4p = pl.BlockSpec((PEER_HEADS, nlt // 2, N_KEYS, 2 * LANES), lambda ti: (0, ti, 0, 0))
    sel_shape = (PEER_HEADS, n // LANES, N_KEYS, LANES)
    selp_shape = (PEER_HEADS, n // (2 * LANES), N_KEYS, 2 * LANES)
    kern = functools.partial(_mix_kernel, tm=tm)
    return pl.pallas_call(
        kern,
        name="mix",
        grid=(n // tm,),
        in_specs=[
            pl.BlockSpec((tm, att.shape[-1]), lambda ti: (ti, 0)),
            pl.BlockSpec((tm, cn.shape[-1]), lambda ti: (ti, 0)),
            pl.BlockSpec((tm, d), lambda ti: (ti, 0)),
            pl.BlockSpec((1, att.shape[-1]), const),
            pl.BlockSpec((d_mix, d), const),
            pl.BlockSpec((1, d), const),
            pl.BlockSpec(w['w_query'].shape, const),
            pl.BlockSpec((N_KEYS, D_KEY // 2), const),
            pl.BlockSpec((N_KEYS, D_KEY // 2), const),
        ],
        out_specs=[
            pl.BlockSpec((tm, d), lambda ti: (ti, 0)),
            pl.BlockSpec((d, tm), lambda ti: (0, ti)),
            tile4, tile4, tile4p, tile4p,
        ],
        out_shape=[
            jax.ShapeDtypeStruct((n, d), F32),
            jax.ShapeDtypeStruct((d, n), BF16),
            jax.ShapeDtypeStruct(sel_shape, F32),
            jax.ShapeDtypeStruct(sel_shape, F32),
            jax.ShapeDtypeStruct(selp_shape, BF16),
            jax.ShapeDtypeStruct(selp_shape, BF16),
        ],
        scratch_shapes=[
            pltpu.VMEM((2 * PEER_HEADS, nlt, N_KEYS, LANES), F32),
            pltpu.VMEM((2 * PEER_HEADS, nlt, N_KEYS, LANES), F32),
            pltpu.VMEM((2 * PEER_HEADS, nlt, PEER_TOPK, LANES), F32),
        ],
        compiler_params=pltpu.CompilerParams(
            dimension_semantics=("arbitrary",), vmem_limit_bytes=VMEM_LIMIT),
    )(att, cn, x2, w['attn_out_g'], w['w_out'], w['norm2_g'], w['w_query'], w['keys1'], w['keys2'])


def _gelu_tanh(x):
    c = float(np.sqrt(2.0 / np.pi))
    hx = 0.5 * x
    return hx + hx * jnp.tanh(x * (c + (c * 0.044715) * (x * x)))


PACK_ROWS = 16


def _peer_kernel(xnt_ref, u_ref, vt_ref, c_ref, e1_ref, r2_ref, e2_ref, h1_ref, fg_ref, y_ref,
                 acc, pbuf, tbl, *, t, ec):
    e = pl.program_id(1)
    nlt = t // LANES
    nib = ec // N_KEYS
    nsub = ec // SUB_EXPERTS
    kps = SUB_EXPERTS // N_KEYS
    npos = N_KEYS // PACK_ROWS

    @pl.when(e == 0)
    def _():
        acc[...] = jnp.zeros_like(acc)

    row0 = pl.multiple_of(e * nib, nib)
    for h in range(PEER_HEADS):
        for lt in range(nlt):
            ls = slice((lt % 2) * LANES, (lt % 2 + 1) * LANES)
            c8 = c_ref[h, lt, pl.ds(row0, nib), :]
            e8 = e1_ref[h, lt, pl.ds(row0, nib), :]
            for kk in range(nib):
                tbl[0, h, lt // 2, kk, :, ls] = jnp.broadcast_to(c8[kk:kk + 1, :], (PACK_ROWS, LANES)).astype(BF16)
                tbl[1, h, lt // 2, kk, :, ls] = jnp.broadcast_to(e8[kk:kk + 1, :], (PACK_ROWS, LANES)).astype(BF16)

    for sb in range(nsub):
        act = _dot(u_ref[sb * SUB_EXPERTS:(sb + 1) * SUB_EXPERTS, :], xnt_ref[...])
        for k in range(kps):
            for lt2 in range(nlt // 2):
                ls = slice(lt2 * 2 * LANES, (lt2 + 1) * 2 * LANES)
                gate = None
                for h in range(PEER_HEADS):
                    cb = jnp.concatenate([tbl[0, h, lt2, sb * kps + k]] * npos, axis=0)
                    eb = jnp.concatenate([tbl[1, h, lt2, sb * kps + k]] * npos, axis=0)
                    term = jnp.where(r2_ref[h, lt2] < cb, e2_ref[h, lt2] * eb, jnp.zeros((), BF16))
                    gate = term if gate is None else gate + term
                gl = _gelu_tanh(act[k * N_KEYS:(k + 1) * N_KEYS, ls]).astype(BF16)
                pbuf[sb, k * N_KEYS:(k + 1) * N_KEYS, ls] = gate * gl
        acc[...] += _dot(vt_ref[sb], pbuf[sb])

    @pl.when(e == pl.num_programs(1) - 1)
    def _():
        h2 = h1_ref[...] + acc[...].T
        y_ref[...] = _rms(h2, fg_ref[...])


def _peer_call(xnt, u, vt, c, e1, r2, e2, h1, fg, t, ec):
    d, n = xnt.shape
    ne = u.shape[0] // ec
    nsub = ec // SUB_EXPERTS
    assert vt.shape == (ne * nsub, d, SUB_EXPERTS)
    nlt = t // LANES
    tile4 = pl.BlockSpec((PEER_HEADS, nlt, N_KEYS, LANES), lambda ti, ei: (0, ti, 0, 0))
    tile4p = pl.BlockSpec((PEER_HEADS, nlt // 2, N_KEYS, 2 * LANES), lambda ti, ei: (0, ti, 0, 0))
    kern = functools.partial(_peer_kernel, t=t, ec=ec)
    return pl.pallas_call(
        kern,
        name="peer",
        grid=(n // t, ne),
        in_specs=[
            pl.BlockSpec((d, t), lambda ti, ei: (0, ti)),
            pl.BlockSpec((ec, d), lambda ti, ei: (ei, 0)),
            pl.BlockSpec((nsub, d, SUB_EXPERTS), lambda ti, ei: (ei, 0, 0)),
            tile4, tile4, tile4p, tile4p,
            pl.BlockSpec((t, d), lambda ti, ei: (ti, 0)),
            pl.BlockSpec((1, d), lambda ti, ei: (0, 0)),
        ],
        out_specs=pl.BlockSpec((t, d), lambda ti, ei: (ti, 0)),
        out_shape=jax.ShapeDtypeStruct((n, d), F32),
        scratch_shapes=[pltpu.VMEM((d, t), F32), pltpu.VMEM((nsub, SUB_EXPERTS, t), BF16),
                        pltpu.VMEM((2, PEER_HEADS, nlt // 2, ec // N_KEYS, PACK_ROWS, 2 * LANES), BF16)],
        compiler_params=pltpu.CompilerParams(
            dimension_semantics=("arbitrary", "arbitrary"), vmem_limit_bytes=VMEM_LIMIT),
    )(xnt, u, vt, c, e1, r2, e2, h1, fg)


def _rope_slab_tables(length):
    inv = 1.0 / (ROPE_THETA ** (jnp.arange(0, QK_ROPE, 2, dtype=F32) / QK_ROPE))
    ang = jnp.arange(length, dtype=F32)[:, None] * inv[None, :]
    cos, sin = jnp.cos(ang), jnp.sin(ang)
    half = QK_ROPE // 2
    pad = HEAD_SLAB - QK_NOPE - QK_ROPE
    ones = jnp.ones((length, QK_NOPE), F32)
    zeros_n = jnp.zeros((length, QK_NOPE), F32)
    zeros_p = jnp.zeros((length, pad), F32)
    del half
    cos_t = jnp.concatenate([ones, cos, cos, zeros_p], axis=1)
    sin_t = jnp.concatenate([zeros_n, -sin, sin, zeros_p], axis=1)
    return cos_t, sin_t


def _head_slabs(wcols, n_heads, width, offset=0):
    k = wcols.shape[0]
    w3 = wcols.reshape(k, n_heads, width)

    def at(off):
        return jnp.pad(w3, ((0, 0), (0, 0), (off, HEAD_SLAB - width - off)))

    if isinstance(offset, int):
        return at(offset).reshape(k, n_heads * HEAD_SLAB)
    out = None
    for off in sorted(set(offset)):
        pick = jnp.asarray([o == off for o in offset])[None, :, None]
        out = jnp.where(pick, at(off), 0.0) if out is None else jnp.where(pick, at(off), out)
    return out.reshape(k, n_heads * HEAD_SLAB)


def _prep_weights(norm1_g, w_in, q_norm_g, w_uq, kv_norm_g, w_ukv, conv_w, conv_b, attn_out_g,
                  conv_out_g, w_out, norm2_g, peer_w_query, peer_keys1, peer_keys2):
    half = QK_ROPE // 2
    o1 = Q_LORA
    o2 = o1 + KV_LORA
    o3 = o2 + QK_ROPE
    k_in = w_in.shape[0]
    kr = w_in[:, o2:o3]
    zpad_n = jnp.zeros((k_in, QK_NOPE), F32)
    zpad_p = jnp.zeros((k_in, HEAD_SLAB - QK_NOPE - QK_ROPE), F32)
    kr_a = jnp.concatenate([zpad_n, kr[:, :half], kr[:, half:], zpad_p], axis=1)
    kr_b = jnp.concatenate([zpad_n, kr[:, half:], kr[:, :half], zpad_p], axis=1)
    w_in_ext = jnp.concatenate([w_in[:, :o2], kr_a, kr_b, w_in[:, o3:]], axis=1).astype(BF16)

    qd = QK_NOPE + QK_ROPE
    wq3 = w_uq.reshape(Q_LORA, ATT_HEADS, qd)
    wq_swap = jnp.concatenate([wq3[:, :, :QK_NOPE], wq3[:, :, QK_NOPE + half:], wq3[:, :, QK_NOPE:QK_NOPE + half]],
                              axis=2).reshape(Q_LORA, ATT_HEADS * qd)
    w_q = jnp.concatenate([_head_slabs(w_uq, ATT_HEADS, qd), _head_slabs(wq_swap, ATT_HEADS, qd)],
                          axis=1).astype(BF16)

    wkv3 = w_ukv.reshape(KV_LORA, ATT_HEADS, QK_NOPE + V_HEAD)
    wk = wkv3[:, :, :QK_NOPE].reshape(KV_LORA, ATT_HEADS * QK_NOPE)
    wv = wkv3[:, :, QK_NOPE:].reshape(KV_LORA, ATT_HEADS * V_HEAD)
    v_off = [V_HEAD * (h % 2) for h in range(ATT_HEADS)]
    w_kv = jnp.concatenate([_head_slabs(wk, ATT_HEADS, QK_NOPE), _head_slabs(wv, ATT_HEADS, V_HEAD, v_off)],
                           axis=1).astype(BF16)
    return {
        'norm1_g': norm1_g.reshape(1, -1), 'w_in': w_in_ext,
        'q_norm_g': q_norm_g.reshape(1, -1), 'w_q': w_q,
        'kv_norm_g': kv_norm_g.reshape(1, -1), 'w_kv': w_kv,
        'conv_w': conv_w, 'conv_b': conv_b.reshape(1, -1), 'conv_out_g': conv_out_g.reshape(1, -1),
        'attn_out_g': attn_out_g.reshape(1, -1), 'w_out': w_out.astype(BF16),
        'norm2_g': norm2_g.reshape(1, -1), 'w_query': peer_w_query.astype(BF16),
        'keys1': peer_keys1.astype(BF16), 'keys2': peer_keys2.astype(BF16),
    }


def kernel(x, meta_tokens, norm1_g, w_in, q_norm_g, w_uq, kv_norm_g, w_ukv, conv_w, conv_b, attn_out_g,
           conv_out_g, w_out, norm2_g, peer_w_query, peer_keys1, peer_keys2, peer_u, peer_v, final_g):
    b, s, d = x.shape
    assert norm1_g.shape[0] == 1, "single-layer kernel"
    assert meta_tokens.shape[0] == N_META
    tm = min(TM_PROJ, s)
    assert s % tm == 0 and s % LANES == 0
    w = _prep_weights(norm1_g[0], w_in[0], q_norm_g[0], w_uq[0], kv_norm_g[0], w_ukv[0], conv_w[0],
                      conv_b[0], attn_out_g[0], conv_out_g[0], w_out[0], norm2_g[0], peer_w_query[0],
                      peer_keys1[0], peer_keys2[0])
    d_conv = conv_b.shape[-1]
    cos_t, sin_t = _rope_slab_tables(N_META + s)

    _, k_m, v_m, _, z_m = _proj_call(meta_tokens[None], jnp.zeros((8, d_conv), F32), w,
                                     cos_t[:N_META], sin_t[:N_META], N_META)
    q, k, v, cn, _ = _proj_call(x, z_m[0], w, cos_t[N_META:], sin_t[N_META:], tm)
    hw = ATT_HEADS * HEAD_SLAB
    km = jnp.pad(k_m[0], ((0, LANES - N_META), (0, 0)))
    vm = jnp.pad(v_m[0], ((0, LANES - N_META), (0, 0)))
    att = _attn_call(q, k, v, km, vm, min(TQ_ATT, s), HEADS_PER_STEP)

    n = b * s
    tmix = min(TM_MIX, n)
    h1, xnt, c, e1, r2, e2 = _mix_call(att.reshape(n, D_ATT), cn.reshape(n, d_conv), x.reshape(n, d), w, tmix)
    u_bf = peer_u[0].astype(BF16)
    vt_bf = peer_v[0].reshape(-1, SUB_EXPERTS, d).transpose(0, 2, 1).astype(BF16)
    y = _peer_call(xnt, u_bf, vt_bf, c, e1, r2, e2, h1, final_g.reshape(1, -1), min(T_PEER, n), E_CHUNK)
    return y.reshape(b, s, d)
```

```python
import functools

import numpy as np
import jax
import jax.numpy as jnp
from jax import lax
from jax.experimental import pallas as pl
from jax.experimental.pallas import tpu as pltpu

F32 = jnp.float32
BF16 = jnp.bfloat16

N_META = 16
EPS = 1e-6
ROPE_THETA = 10000.0
ATT_HEADS = 8
Q_LORA = 256
KV_LORA = 128
QK_NOPE = 64
QK_ROPE = 32
V_HEAD = 64
D_ATT = ATT_HEADS * V_HEAD
PEER_HEADS = 8
N_KEYS = 128
D_KEY = 256
PEER_TOPK = 16
CONV_WIDTH = 3

LANES = 128
HEAD_SLAB = LANES
NEG_BIG = -1e30
LOG2E = 1.4426950408889634
VMEM_LIMIT = 56 * 1024 * 1024

TM_PROJ = 512
TQ_ATT = 512
TK_ATT = 512
HEADS_PER_STEP = 4
TM_MIX = 512
T_PEER = 512
E_CHUNK = 2048
SUB_EXPERTS = 512


def _dot(a, b):
    return jnp.dot(a, b, preferred_element_type=F32)


def _dot_nt(a, b):
    return lax.dot_general(a, b, (((1,), (1,)), ((), ())), preferred_element_type=F32)


def _rms(x, g):
    y = x * lax.rsqrt(jnp.mean(x * x, axis=-1, keepdims=True) + EPS)
    return y * g


def _proj_kernel(x_ref, zprev_ref, g1_ref, win_ref, qg_ref, wq_ref, kvg_ref, wkv_ref, cw_ref, cb_ref,
                 cg_ref, cos_ref, sin_ref, vone_ref,
                 q_ref, k_ref, v_ref, cn_ref, zt_ref, zbuf, *, tm, d_conv, q_scale):
    t = pl.program_id(1)

    @pl.when(t == 0)
    def _():
        zbuf[0:8, :] = zprev_ref[...]

    x = x_ref[0]
    hn = _rms(x, g1_ref[...]).astype(BF16)
    p = _dot(hn, win_ref[...])
    cos = cos_ref[...]
    sin = sin_ref[...]

    qn = _rms(p[:, :Q_LORA], qg_ref[...]).astype(BF16)
    qq = _dot(qn, wq_ref[...])
    hw = ATT_HEADS * HEAD_SLAB
    for h in range(ATT_HEADS):
        lo = h * HEAD_SLAB
        qh = qq[:, lo:lo + HEAD_SLAB] * cos + qq[:, hw + lo:hw + lo + HEAD_SLAB] * sin
        q_ref[0, :, lo:lo + HEAD_SLAB] = (qh * q_scale).astype(BF16)

    o1 = Q_LORA
    o2 = o1 + KV_LORA
    kvn = _rms(p[:, o1:o2], kvg_ref[...]).astype(BF16)
    kv = _dot(kvn, wkv_ref[...])
    kr = p[:, o2:o2 + HEAD_SLAB] * cos + p[:, o2 + HEAD_SLAB:o2 + 2 * HEAD_SLAB] * sin
    for h in range(ATT_HEADS):
        lo = h * HEAD_SLAB
        k_ref[0, :, lo:lo + HEAD_SLAB] = (kv[:, lo:lo + HEAD_SLAB] + kr).astype(BF16)
    v_ref[0] = (kv[:, hw:] + vone_ref[...]).astype(BF16)

    o3 = o2 + 2 * HEAD_SLAB
    bg = p[:, o3:o3 + d_conv]
    z = p[:, o3 + d_conv:o3 + 2 * d_conv] * p[:, o3 + 2 * d_conv:o3 + 3 * d_conv]
    zbuf[8:8 + tm, :] = z
    conv = (cb_ref[...] + cw_ref[0:1, :] * zbuf[6:6 + tm, :] + cw_ref[1:2, :] * zbuf[7:7 + tm, :]
            + cw_ref[2:3, :] * z)
    cn_ref[0] = _rms(bg * conv, cg_ref[...]).astype(BF16)
    tail = zbuf[tm:tm + 8, :]
    zt_ref[0] = tail
    zbuf[0:8, :] = tail


def _proj_call(x, zprev, w, cos, sin, tm):
    b, s, d = x.shape
    d_conv = w['conv_b'].shape[-1]
    hw = ATT_HEADS * HEAD_SLAB
    nt = s // tm
    q_scale = float((QK_NOPE + QK_ROPE) ** -0.5 * LOG2E)
    const = lambda bi, ti: (0, 0)
    kern = functools.partial(_proj_kernel, tm=tm, d_conv=d_conv, q_scale=q_scale)
    return pl.pallas_call(
        kern,
        name="proj",
        grid=(b, nt),
        in_specs=[
            pl.BlockSpec((1, tm, d), lambda bi, ti: (bi, ti, 0)),
            pl.BlockSpec((8, d_conv), const),
            pl.BlockSpec((1, d), const),
            pl.BlockSpec(w['w_in'].shape, const),
            pl.BlockSpec((1, Q_LORA), const),
            pl.BlockSpec(w['w_q'].shape, const),
            pl.BlockSpec((1, KV_LORA), const),
            pl.BlockSpec(w['w_kv'].shape, const),
            pl.BlockSpec((CONV_WIDTH, d_conv), const),
            pl.BlockSpec((1, d_conv), const),
            pl.BlockSpec((1, d_conv), const),
            pl.BlockSpec((tm, HEAD_SLAB), lambda bi, ti: (ti, 0)),
            pl.BlockSpec((tm, HEAD_SLAB), lambda bi, ti: (ti, 0)),
            pl.BlockSpec((1, hw), const),
        ],
        out_specs=[
            pl.BlockSpec((1, tm, hw), lambda bi, ti: (bi, ti, 0)),
            pl.BlockSpec((1, tm, hw), lambda bi, ti: (bi, ti, 0)),
            pl.BlockSpec((1, tm, hw), lambda bi, ti: (bi, ti, 0)),
            pl.BlockSpec((1, tm, d_conv), lambda bi, ti: (bi, ti, 0)),
            pl.BlockSpec((1, 8, d_conv), lambda bi, ti: (bi, ti, 0)),
        ],
        out_shape=[
            jax.ShapeDtypeStruct((b, s, hw), BF16),
            jax.ShapeDtypeStruct((b, s, hw), BF16),
            jax.ShapeDtypeStruct((b, s, hw), BF16),
            jax.ShapeDtypeStruct((b, s, d_conv), BF16),
            jax.ShapeDtypeStruct((b, nt * 8, d_conv), F32),
        ],
        scratch_shapes=[pltpu.VMEM((tm + 8, d_conv), F32)],
        compiler_params=pltpu.CompilerParams(
            dimension_semantics=("arbitrary", "arbitrary"), vmem_limit_bytes=VMEM_LIMIT),
    )(x, zprev, w['norm1_g'], w['w_in'], w['q_norm_g'], w['w_q'], w['kv_norm_g'], w['w_kv'],
      w['conv_w'], w['conv_b'], w['conv_out_g'], cos, sin, w['v_one'])


def _v_lane(h):
    return V_HEAD * (h % 2)


def _one_lane(h):
    return V_HEAD * ((h + 1) % 2)


def _attn_kernel(q_ref, k_ref, v_ref, km_ref, vm_ref, o_ref, *, tq, tk, hg):
    i = pl.program_id(2)
    nd = tq // tk
    row = lax.broadcasted_iota(jnp.int32, (tq, tk), 0)
    col = lax.broadcasted_iota(jnp.int32, (tq, tk), 1)
    meta_ok = lax.broadcasted_iota(jnp.int32, (tq, LANES), 1) < N_META

    def step(carry, qh, kc, vc, mask):
        m, acc = carry
        s = _dot_nt(qh, kc)
        if mask is not None:
            s = jnp.where(mask, s, NEG_BIG)
        m_new = jnp.maximum(m, jnp.max(s, axis=-1, keepdims=True))
        pr = jnp.exp2(s - m_new)
        acc = jnp.exp2(m - m_new) * acc + _dot(pr.astype(BF16), vc)
        return m_new, acc

    slabs = [slice(hh * HEAD_SLAB, (hh + 1) * HEAD_SLAB) for hh in range(hg)]
    qs = [q_ref[0, :, sl] for sl in slabs]
    init = (jnp.full((tq, 1), NEG_BIG, F32), jnp.zeros((tq, LANES), F32))
    carries = tuple(step(init, qs[hh], km_ref[:, slabs[hh]], vm_ref[:, slabs[hh]], meta_ok)
                    for hh in range(hg))

    def body(j, cs):
        rows = pl.ds(pl.multiple_of(j * tk, tk), tk)
        return tuple(step(cs[hh], qs[hh], k_ref[0, rows, slabs[hh]], v_ref[0, rows, slabs[hh]], None)
                     for hh in range(hg))

    carries = lax.fori_loop(0, i * nd, body, carries)
    for c in range(nd):
        rows = pl.ds(pl.multiple_of(i * tq + c * tk, tk), tk)
        causal = col + c * tk <= row
        carries = tuple(step(carries[hh], qs[hh], k_ref[0, rows, slabs[hh]], v_ref[0, rows, slabs[hh]],
                             causal) for hh in range(hg))
    lane = lax.broadcasted_iota(jnp.int32, (tq, LANES), 1)
    outs = []
    for hh in range(hg):
        one = _one_lane(hh)
        acc = carries[hh][1]
        outs.append(jnp.where(lane == one, 0.0, acc / acc[:, one:one + 1]))
    o_ref[0] = jnp.concatenate([outs[2 * p] + outs[2 * p + 1] for p in range(hg // 2)], axis=-1).astype(BF16)


def _attn_call(q, k, v, km, vm, tq, hg):
    b, s, hw = q.shape
    gw = hg * HEAD_SLAB
    ow = (hg // 2) * LANES
    tk = min(TK_ATT, tq)
    kern = functools.partial(_attn_kernel, tq=tq, tk=tk, hg=hg)
    return pl.pallas_call(
        kern,
        name="attn",
        grid=(b, ATT_HEADS // hg, s // tq),
        in_specs=[
            pl.BlockSpec((1, tq, gw), lambda bi, gi, qi: (bi, qi, gi)),
            pl.BlockSpec((1, s, gw), lambda bi, gi, qi: (bi, 0, gi)),
            pl.BlockSpec((1, s, gw), lambda bi, gi, qi: (bi, 0, gi)),
            pl.BlockSpec((LANES, gw), lambda bi, gi, qi: (0, gi)),
            pl.BlockSpec((LANES, gw), lambda bi, gi, qi: (0, gi)),
        ],
        out_specs=pl.BlockSpec((1, tq, ow), lambda bi, gi, qi: (bi, qi, gi)),
        out_shape=jax.ShapeDtypeStruct((b, s, D_ATT), BF16),
        compiler_params=pltpu.CompilerParams(
            dimension_semantics=("arbitrary", "arbitrary", "arbitrary"), vmem_limit_bytes=VMEM_LIMIT),
    )(q, k, v, km, vm)


def _rank16_by_extraction(s):
    key = lax.broadcasted_iota(jnp.int32, s.shape, 0)
    rank = jnp.full(s.shape, float(PEER_TOPK), F32)
    for r in range(PEER_TOPK):
        m = jnp.max(s, axis=0, keepdims=True)
        first = jnp.min(jnp.where(s == m, key, N_KEYS), axis=0, keepdims=True)
        hit = key == first
        rank = jnp.where(hit, float(r), rank)
        s = jnp.where(hit, -jnp.inf, s)
    return rank


def _oddeven_merge(lo, hi, r):
    step = r * 2
    if step < hi - lo:
        yield from _oddeven_merge(lo, hi, step)
        yield from _oddeven_merge(lo + r, hi, step)
        yield from [(i, i + r) for i in range(lo + r, hi - r, step)]
    else:
        yield (lo, lo + r)


def _oddeven_merge_sort(lo, hi):
    if hi - lo >= 1:
        mid = lo + (hi - lo) // 2
        yield from _oddeven_merge_sort(lo, mid)
        yield from _oddeven_merge_sort(mid + 1, hi)
        yield from _oddeven_merge(lo, hi, 1)


_SORT16_NET = tuple(_oddeven_merge_sort(0, PEER_TOPK - 1))
SUBLANES = 8


def _sorted_top16(tiles):
    cols = list(tiles) + [None] * (PEER_TOPK - len(tiles))

    def vmax(x, y):
        return y if x is None else x if y is None else jnp.maximum(x, y)

    def vmin(x, y):
        return None if x is None or y is None else jnp.minimum(x, y)

    def exchange(i, j):
        cols[i], cols[j] = vmax(cols[i], cols[j]), vmin(cols[i], cols[j])

    for i, j in _SORT16_NET:
        exchange(i, j)
    for shift in (4, 2, 1):
        partner = [None if c is None else pltpu.roll(c, shift, axis=0) for c in reversed(cols)]
        cols = [vmax(cols[r], partner[r]) for r in range(PEER_TOPK)]
        for stride in (8, 4, 2, 1):
            for r in range(PEER_TOPK):
                if r & stride == 0:
                    exchange(r, r + stride)
    return cols


def _top16_tile(s):
    nv = N_KEYS // SUBLANES
    rows = [s[SUBLANES * v:SUBLANES * (v + 1), :] for v in range(nv)]
    top = _sorted_top16(rows)
    vals = jnp.concatenate([t[0:1, :] for t in top], axis=0)
    at_least = jnp.zeros((SUBLANES, LANES), F32)
    for v in range(nv):
        at_least = at_least + jnp.where(rows[v] >= top[PEER_TOPK - 1], 1.0, 0.0)
    tied = jnp.where(jnp.sum(at_least, axis=0, keepdims=True) != float(PEER_TOPK), 1.0, 0.0)
    for r in range(PEER_TOPK - 1):
        tied = tied + jnp.where(top[r][0:1, :] <= top[r + 1][0:1, :], 1.0, 0.0)

    def by_count(x):
        out = []
        for v in range(nv):
            rv = jnp.zeros((SUBLANES, LANES), F32)
            for r in range(PEER_TOPK):
                rv = rv + jnp.where(top[r] > x[SUBLANES * v:SUBLANES * (v + 1), :], 1.0, 0.0)
            out.append(rv)
        return jnp.concatenate(out, axis=0)

    rank = lax.cond(jnp.max(tied) > 0.0, _rank16_by_extraction, by_count, s)
    return rank, vals


def _candidate_blocks():
    blocks = []
    for a in range(8):
        nb = PEER_TOPK // (a + 1)
        for b0 in range(0, nb, 8):
            nv = min(8, nb - b0)
            flat = np.array([a * PEER_TOPK + b0 + r for r in range(8)])
            blocks.append(('row', a, b0, nv, flat))
    flat = np.array([(8 + r) * PEER_TOPK for r in range(8)])
    blocks.append(('col', 8, 0, 8, flat))
    return blocks


_CAND_BLOCKS = _candidate_blocks()


def _select_counts(v1, v2):
    sub = lax.broadcasted_iota(jnp.int32, (8, LANES), 0)
    vals = []
    for kind, a0, b0, nv, _ in _CAND_BLOCKS:
        if kind == 'row':
            blk = v1[a0:a0 + 1, :] + v2[b0:b0 + 8, :]
        else:
            blk = v1[a0:a0 + 8, :] + v2[b0:b0 + 1, :]
        if nv < 8:
            blk = jnp.where(sub < nv, blk, -jnp.inf)
        vals.append(blk)
    def by_pairwise_rank(blocks):
        beaten = [jnp.zeros((8, LANES), F32) for _ in blocks]
        for si, (_, _, _, nv_s, flat_s) in enumerate(_CAND_BLOCKS):
            for r in range(nv_s):
                src = jnp.broadcast_to(blocks[si][r:r + 1, :], (8, LANES))
                for ti, (_, _, _, _, flat_t) in enumerate(_CAND_BLOCKS):
                    wins_tie = flat_s[r] < flat_t
                    gt = jnp.where(src > blocks[ti], 1.0, 0.0)
                    ge = jnp.where(src >= blocks[ti], 1.0, 0.0)
                    if wins_tie.all():
                        beats = ge
                    elif not wins_tie.any():
                        beats = gt
                    else:
                        beats = jnp.where(sub >= int((~wins_tie).sum()), ge, gt)
                    beaten[ti] = beaten[ti] + beats
        return tuple(jnp.where(b < float(PEER_TOPK), 1.0, 0.0) for b in beaten)

    kth = _sorted_top16(vals)[PEER_TOPK - 1]
    reach = tuple(jnp.where(blk >= kth, 1.0, 0.0) for blk in vals)
    n_reach = jnp.sum(sum(reach[1:], reach[0]), axis=0, keepdims=True)
    boundary_tie = jnp.max(jnp.where(n_reach != float(PEER_TOPK), 1.0, 0.0)) > 0.0
    selected = lax.cond(boundary_tie, by_pairwise_rank, lambda blocks: reach, tuple(vals))

    top = vals[0][0:1, :]
    z = jnp.zeros((1, LANES), F32)
    cnt_rows = [jnp.zeros((1, LANES), F32) for _ in range(PEER_TOPK)]
    for ti, (kind, a0, b0, nv, _) in enumerate(_CAND_BLOCKS):
        self_f = selected[ti]
        if nv < 8:
            self_f = jnp.where(sub < nv, self_f, 0.0)
        z = z + jnp.sum(self_f * jnp.exp(vals[ti] - top), axis=0, keepdims=True)
        if kind == 'row':
            cnt_rows[a0] = cnt_rows[a0] + jnp.sum(self_f, axis=0, keepdims=True)
        else:
            for r in range(8):
                cnt_rows[a0 + r] = cnt_rows[a0 + r] + self_f[r:r + 1, :]
    return jnp.concatenate(cnt_rows, axis=0), z


def _mix_kernel(att_ref, cn_ref, x_ref, ag_ref, wo_ref, g2_ref, wqr_ref, k1_ref, k2_ref,
                h1_ref, xnt_ref, c_ref, e1_ref, r2_ref, e2_ref, st, rk, vs, *, tm):
    nlt = tm // LANES
    att = _rms(att_ref[...].astype(F32), ag_ref[...]).astype(BF16)
    d_att = att.shape[-1]
    h1 = x_ref[...] + _dot(att, wo_ref[0:d_att, :]) + _dot(cn_ref[...], wo_ref[d_att:, :])
    h1_ref[...] = h1
    xn32 = _rms(h1, g2_ref[...])
    xn = xn32.astype(BF16)
    xnt_ref[...] = xn32.T.astype(BF16)
    qp = _dot(xn, wqr_ref[...]).astype(BF16)
    half = D_KEY // 2
    for h in range(PEER_HEADS):
        for side, kref in enumerate((k1_ref, k2_ref)):
            lo = h * D_KEY + side * half
            sc = _dot_nt(kref[...], qp[:, lo:lo + half])
            for lt in range(nlt):
                st[2 * h + side, lt] = sc[:, lt * LANES:(lt + 1) * LANES]

    def rank_body(idx, carry):
        p = idx // nlt
        lt = idx % nlt
        rank, vals = _top16_tile(st[p, lt])
        rk[p, lt] = rank
        vs[p, lt] = vals
        return carry

    lax.fori_loop(0, 2 * PEER_HEADS * nlt, rank_body, 0)

    def sel_body(idx, carry):
        h = idx // (nlt // 2)
        lt2 = idx % (nlt // 2)
        for half in range(2):
            lt = 2 * lt2 + half
            ls = slice(half * LANES, (half + 1) * LANES)
            v1 = vs[2 * h, lt]
            v2 = vs[2 * h + 1, lt]
            cnt, z = _select_counts(v1, v2)
            r1 = rk[2 * h, lt]
            c = jnp.zeros((N_KEYS, LANES), F32)
            for a in range(PEER_TOPK):
                c = c + jnp.where(r1 == float(a), cnt[a:a + 1, :], 0.0)
            c_ref[h, lt] = c
            e1_ref[h, lt] = jnp.exp(st[2 * h, lt] - v1[0:1, :]) * (1.0 / z)
            r2_ref[h, lt2, :, ls] = rk[2 * h + 1, lt].astype(BF16)
            e2_ref[h, lt2, :, ls] = jnp.exp(st[2 * h + 1, lt] - v2[0:1, :]).astype(BF16)
        return carry

    lax.fori_loop(0, PEER_HEADS * (nlt // 2), sel_body, 0)


def _mix_call(att, cn, x2, w, tm):
    n, d = x2.shape
    nlt = tm // LANES
    d_mix = w['w_out'].shape[0]
    const = lambda ti: (0, 0)
    tile4 = pl.BlockSpec((PEER_HEADS, nlt, N_KEYS, LANES), lambda ti: (0, ti, 0, 0))
    tile4p = pl.BlockSpec((PEER_HEADS, nlt // 2, N_KEYS, 2 * LANES), lambda ti: (0, ti, 0, 0))
    sel_shape = (PEER_HEADS, n // LANES, N_KEYS, LANES)
    selp_shape = (PEER_HEADS, n // (2 * LANES), N_KEYS, 2 * LANES)
    kern = functools.partial(_mix_kernel, tm=tm)
    return pl.pallas_call(
        kern,
        name="mix",
        grid=(n // tm,),
        in_specs=[
            pl.BlockSpec((tm, att.shape[-1]), lambda ti: (ti, 0)),
            pl.BlockSpec((tm, cn.shape[-1]), lambda ti: (ti, 0)),
            pl.BlockSpec((tm, d), lambda ti: (ti, 0)),
            pl.BlockSpec((1, att.shape[-1]), const),
            pl.BlockSpec((d_mix, d), const),
            pl.BlockSpec((1, d), const),
            pl.BlockSpec(w['w_query'].shape, const),
            pl.BlockSpec((N_KEYS, D_KEY // 2), const),
            pl.BlockSpec((N_KEYS, D_KEY // 2), const),
        ],
        out_specs=[
            pl.BlockSpec((tm, d), lambda ti: (ti, 0)),
            pl.BlockSpec((d, tm), lambda ti: (0, ti)),
            tile4, tile4, tile4p, tile4p,
        ],
        out_shape=[
            jax.ShapeDtypeStruct((n, d), F32),
            jax.ShapeDtypeStruct((d, n), BF16),
            jax.ShapeDtypeStruct(sel_shape, F32),
            jax.ShapeDtypeStruct(sel_shape, F32),
            jax.ShapeDtypeStruct(selp_shape, BF16),
            jax.ShapeDtypeStruct(selp_shape, BF16),
        ],
        scratch_shapes=[
            pltpu.VMEM((2 * PEER_HEADS, nlt, N_KEYS, LANES), F32),
            pltpu.VMEM((2 * PEER_HEADS, nlt, N_KEYS, LANES), F32),
            pltpu.VMEM((2 * PEER_HEADS, nlt, PEER_TOPK, LANES), F32),
        ],
        compiler_params=pltpu.CompilerParams(
            dimension_semantics=("arbitrary",), vmem_limit_bytes=VMEM_LIMIT),
    )(att, cn, x2, w['attn_out_g'], w['w_out'], w['norm2_g'], w['w_query'], w['keys1'], w['keys2'])


def _gelu_tanh(x):
    c = float(np.sqrt(2.0 / np.pi))
    hx = 0.5 * x
    return hx + hx * jnp.tanh(x * (c + (c * 0.044715) * (x * x)))


PACK_ROWS = 16


def _peer_kernel(xnt_ref, u_ref, vt_ref, c_ref, e1_ref, r2_ref, e2_ref, h1_ref, fg_ref, y_ref,
                 acc, pbuf, tbl, *, t, ec):
    e = pl.program_id(1)
    nlt = t // LANES
    nib = ec // N_KEYS
    nsub = ec // SUB_EXPERTS
    kps = SUB_EXPERTS // N_KEYS
    npos = N_KEYS // PACK_ROWS

    @pl.when(e == 0)
    def _():
        acc[...] = jnp.zeros_like(acc)

    row0 = pl.multiple_of(e * nib, nib)
    for h in range(PEER_HEADS):
        for lt in range(nlt):
            ls = slice((lt % 2) * LANES, (lt % 2 + 1) * LANES)
            c8 = c_ref[h, lt, pl.ds(row0, nib), :]
            e8 = e1_ref[h, lt, pl.ds(row0, nib), :]
            for kk in range(nib):
                tbl[0, h, lt // 2, kk, :, ls] = jnp.broadcast_to(c8[kk:kk + 1, :], (PACK_ROWS, LANES)).astype(BF16)
                tbl[1, h, lt // 2, kk, :, ls] = jnp.broadcast_to(e8[kk:kk + 1, :], (PACK_ROWS, LANES)).astype(BF16)

    for sb in range(nsub):
        act = _dot(u_ref[sb * SUB_EXPERTS:(sb + 1) * SUB_EXPERTS, :], xnt_ref[...])
        for k in range(kps):
            for lt2 in range(nlt // 2):
                ls = slice(lt2 * 2 * LANES, (lt2 + 1) * 2 * LANES)
                gate = None
                for h in range(PEER_HEADS):
                    cb = jnp.concatenate([tbl[0, h, lt2, sb * kps + k]] * npos, axis=0)
                    eb = jnp.concatenate([tbl[1, h, lt2, sb * kps + k]] * npos, axis=0)
                    term = jnp.where(r2_ref[h, lt2] < cb, e2_ref[h, lt2] * eb, jnp.zeros((), BF16))
                    gate = term if gate is None else gate + term
                gl = _gelu_tanh(act[k * N_KEYS:(k + 1) * N_KEYS, ls]).astype(BF16)
                pbuf[sb, k * N_KEYS:(k + 1) * N_KEYS, ls] = gate * gl
        acc[...] += _dot(vt_ref[sb], pbuf[sb])

    @pl.when(e == pl.num_programs(1) - 1)
    def _():
        h2 = h1_ref[...] + acc[...].T
        y_ref[...] = _rms(h2, fg_ref[...])


def _peer_call(xnt, u, vt, c, e1, r2, e2, h1, fg, t, ec):
    d, n = xnt.shape
    ne = u.shape[0] // ec
    nsub = ec // SUB_EXPERTS
    assert vt.shape == (ne * nsub, d, SUB_EXPERTS)
    nlt = t // LANES
    tile4 = pl.BlockSpec((PEER_HEADS, nlt, N_KEYS, LANES), lambda ti, ei: (0, ti, 0, 0))
    tile4p = pl.BlockSpec((PEER_HEADS, nlt // 2, N_KEYS, 2 * LANES), lambda ti, ei: (0, ti, 0, 0))
    kern = functools.partial(_peer_kernel, t=t, ec=ec)
    return pl.pallas_call(
        kern,
        name="peer",
        grid=(n // t, ne),
        in_specs=[
            pl.BlockSpec((d, t), lambda ti, ei: (0, ti)),
            pl.BlockSpec((ec, d), lambda ti, ei: (ei, 0)),
            pl.BlockSpec((nsub, d, SUB_EXPERTS), lambda ti, ei: (ei, 0, 0)),
            tile4, tile4, tile4p, tile4p,
            pl.BlockSpec((t, d), lambda ti, ei: (ti, 0)),
            pl.BlockSpec((1, d), lambda ti, ei: (0, 0)),
        ],
        out_specs=pl.BlockSpec((t, d), lambda ti, ei: (ti, 0)),
        out_shape=jax.ShapeDtypeStruct((n, d), F32),
        scratch_shapes=[pltpu.VMEM((d, t), F32), pltpu.VMEM((nsub, SUB_EXPERTS, t), BF16),
                        pltpu.VMEM((2, PEER_HEADS, nlt // 2, ec // N_KEYS, PACK_ROWS, 2 * LANES), BF16)],
        compiler_params=pltpu.CompilerParams(
            dimension_semantics=("arbitrary", "arbitrary"), vmem_limit_bytes=VMEM_LIMIT),
    )(xnt, u, vt, c, e1, r2, e2, h1, fg)


def _rope_slab_tables(length):
    inv = 1.0 / (ROPE_THETA ** (jnp.arange(0, QK_ROPE, 2, dtype=F32) / QK_ROPE))
    ang = jnp.arange(length, dtype=F32)[:, None] * inv[None, :]
    cos, sin = jnp.cos(ang), jnp.sin(ang)
    half = QK_ROPE // 2
    pad = HEAD_SLAB - QK_NOPE - QK_ROPE
    ones = jnp.ones((length, QK_NOPE), F32)
    zeros_n = jnp.zeros((length, QK_NOPE), F32)
    zeros_p = jnp.zeros((length, pad), F32)
    del half
    cos_t = jnp.concatenate([ones, cos, cos, zeros_p], axis=1)
    sin_t = jnp.concatenate([zeros_n, -sin, sin, zeros_p], axis=1)
    return cos_t, sin_t


def _head_slabs(wcols, n_heads, width, offset=0):
    k = wcols.shape[0]
    w3 = wcols.reshape(k, n_heads, width)

    def at(off):
        return jnp.pad(w3, ((0, 0), (0, 0), (off, HEAD_SLAB - width - off)))

    if isinstance(offset, int):
        return at(offset).reshape(k, n_heads * HEAD_SLAB)
    out = None
    for off in sorted(set(offset)):
        pick = jnp.asarray([o == off for o in offset])[None, :, None]
        out = jnp.where(pick, at(off), 0.0) if out is None else jnp.where(pick, at(off), out)
    return out.reshape(k, n_heads * HEAD_SLAB)


def _prep_weights(norm1_g, w_in, q_norm_g, w_uq, kv_norm_g, w_ukv, conv_w, conv_b, attn_out_g,
                  conv_out_g, w_out, norm2_g, peer_w_query, peer_keys1, peer_keys2):
    half = QK_ROPE // 2
    o1 = Q_LORA
    o2 = o1 + KV_LORA
    o3 = o2 + QK_ROPE
    k_in = w_in.shape[0]
    kr = w_in[:, o2:o3]
    zpad_n = jnp.zeros((k_in, QK_NOPE), F32)
    zpad_p = jnp.zeros((k_in, HEAD_SLAB - QK_NOPE - QK_ROPE), F32)
    kr_a = jnp.concatenate([zpad_n, kr[:, :half], kr[:, half:], zpad_p], axis=1)
    kr_b = jnp.concatenate([zpad_n, kr[:, half:], kr[:, :half], zpad_p], axis=1)
    w_in_ext = jnp.concatenate([w_in[:, :o2], kr_a, kr_b, w_in[:, o3:]], axis=1).astype(BF16)

    qd = QK_NOPE + QK_ROPE
    wq3 = w_uq.reshape(Q_LORA, ATT_HEADS, qd)
    wq_swap = jnp.concatenate([wq3[:, :, :QK_NOPE], wq3[:, :, QK_NOPE + half:], wq3[:, :, QK_NOPE:QK_NOPE + half]],
                              axis=2).reshape(Q_LORA, ATT_HEADS * qd)
    w_q = jnp.concatenate([_head_slabs(w_uq, ATT_HEADS, qd), _head_slabs(wq_swap, ATT_HEADS, qd)],
                          axis=1).astype(BF16)

    wkv3 = w_ukv.reshape(KV_LORA, ATT_HEADS, QK_NOPE + V_HEAD)
    wk = wkv3[:, :, :QK_NOPE].reshape(KV_LORA, ATT_HEADS * QK_NOPE)
    wv = wkv3[:, :, QK_NOPE:].reshape(KV_LORA, ATT_HEADS * V_HEAD)
    v_off = [_v_lane(h) for h in range(ATT_HEADS)]
    w_kv = jnp.concatenate([_head_slabs(wk, ATT_HEADS, QK_NOPE), _head_slabs(wv, ATT_HEADS, V_HEAD, v_off)],
                           axis=1).astype(BF16)
    one_cols = np.zeros((1, ATT_HEADS * HEAD_SLAB), np.float32)
    for h in range(ATT_HEADS):
        one_cols[0, h * HEAD_SLAB + _one_lane(h)] = 1.0
    return {
        'v_one': jnp.asarray(one_cols),
        'norm1_g': norm1_g.reshape(1, -1), 'w_in': w_in_ext,
        'q_norm_g': q_norm_g.reshape(1, -1), 'w_q': w_q,
        'kv_norm_g': kv_norm_g.reshape(1, -1), 'w_kv': w_kv,
        'conv_w': conv_w, 'conv_b': conv_b.reshape(1, -1), 'conv_out_g': conv_out_g.reshape(1, -1),
        'attn_out_g': attn_out_g.reshape(1, -1), 'w_out': w_out.astype(BF16),
        'norm2_g': norm2_g.reshape(1, -1), 'w_query': peer_w_query.astype(BF16),
        'keys1': peer_keys1.astype(BF16), 'keys2': peer_keys2.astype(BF16),
    }


def kernel(x, meta_tokens, norm1_g, w_in, q_norm_g, w_uq, kv_norm_g, w_ukv, conv_w, conv_b, attn_out_g,
           conv_out_g, w_out, norm2_g, peer_w_query, peer_keys1, peer_keys2, peer_u, peer_v, final_g):
    b, s, d = x.shape
    assert norm1_g.shape[0] == 1, "single-layer kernel"
    assert meta_tokens.shape[0] == N_META
    tm = min(TM_PROJ, s)
    assert s % tm == 0 and s % LANES == 0
    w = _prep_weights(norm1_g[0], w_in[0], q_norm_g[0], w_uq[0], kv_norm_g[0], w_ukv[0], conv_w[0],
                      conv_b[0], attn_out_g[0], conv_out_g[0], w_out[0], norm2_g[0], peer_w_query[0],
                      peer_keys1[0], peer_keys2[0])
    d_conv = conv_b.shape[-1]
    cos_t, sin_t = _rope_slab_tables(N_META + s)

    _, k_m, v_m, _, z_m = _proj_call(meta_tokens[None], jnp.zeros((8, d_conv), F32), w,
                                     cos_t[:N_META], sin_t[:N_META], N_META)
    q, k, v, cn, _ = _proj_call(x, z_m[0], w, cos_t[N_META:], sin_t[N_META:], tm)
    hw = ATT_HEADS * HEAD_SLAB
    km = jnp.pad(k_m[0], ((0, LANES - N_META), (0, 0)))
    vm = jnp.pad(v_m[0], ((0, LANES - N_META), (0, 0)))
    att = _attn_call(q, k, v, km, vm, min(TQ_ATT, s), HEADS_PER_STEP)

    n = b * s
    tmix = min(TM_MIX, n)
    h1, xnt, c, e1, r2, e2 = _mix_call(att.reshape(n, D_ATT), cn.reshape(n, d_conv), x.reshape(n, d), w, tmix)
    u_bf = peer_u[0].astype(BF16)
    vt_bf = peer_v[0].reshape(-1, SUB_EXPERTS, d).transpose(0, 2, 1).astype(BF16)
    y = _peer_call(xnt, u_bf, vt_bf, c, e1, r2, e2, h1, final_g.reshape(1, -1), min(T_PEER, n), E_CHUNK)
    return y.reshape(b, s, d)
```

```python
import functools

import numpy as np
import jax
import jax.numpy as jnp
from jax import lax
from jax.experimental import pallas as pl
from jax.experimental.pallas import tpu as pltpu

F32 = jnp.float32
BF16 = jnp.bfloat16

N_META = 16
EPS = 1e-6
ROPE_THETA = 10000.0
ATT_HEADS = 8
Q_LORA = 256
KV_LORA = 128
QK_NOPE = 64
QK_ROPE = 32
V_HEAD = 64
D_ATT = ATT_HEADS * V_HEAD
PEER_HEADS = 8
N_KEYS = 128
D_KEY = 256
PEER_TOPK = 16
CONV_WIDTH = 3

LANES = 128
HEAD_SLAB = LANES
NEG_BIG = -1e30
LOG2E = 1.4426950408889634
VMEM_LIMIT = 56 * 1024 * 1024

TM_PROJ = 512
TQ_ATT = 512
TK_ATT = 512
HEADS_PER_STEP = 4
TM_MIX = 512
T_PEER = 512
E_CHUNK = 2048
SUB_EXPERTS = 512


def _dot(a, b):
    return jnp.dot(a, b, preferred_element_type=F32)


def _dot_nt(a, b):
    return lax.dot_general(a, b, (((1,), (1,)), ((), ())), preferred_element_type=F32)


def _rms(x, g):
    y = x * lax.rsqrt(jnp.mean(x * x, axis=-1, keepdims=True) + EPS)
    return y * g


def _proj_kernel(x_ref, zprev_ref, g1_ref, win_ref, qg_ref, wq_ref, kvg_ref, wkv_ref, cw_ref, cb_ref,
                 cg_ref, cos_ref, sin_ref, vone_ref,
                 q_ref, k_ref, v_ref, cn_ref, zt_ref, zbuf, *, tm, d_conv, q_scale):
    t = pl.program_id(1)

    @pl.when(t == 0)
    def _():
        zbuf[0:8, :] = zprev_ref[...]

    x = x_ref[0]
    hn = _rms(x, g1_ref[...]).astype(BF16)
    p = _dot(hn, win_ref[...])
    cos = cos_ref[...]
    sin = sin_ref[...]

    qn = _rms(p[:, :Q_LORA], qg_ref[...]).astype(BF16)
    qq = _dot(qn, wq_ref[...])
    hw = ATT_HEADS * HEAD_SLAB
    for h in range(ATT_HEADS):
        lo = h * HEAD_SLAB
        qh = qq[:, lo:lo + HEAD_SLAB] * cos + qq[:, hw + lo:hw + lo + HEAD_SLAB] * sin
        q_ref[0, :, lo:lo + HEAD_SLAB] = (qh * q_scale).astype(BF16)

    o1 = Q_LORA
    o2 = o1 + KV_LORA
    kvn = _rms(p[:, o1:o2], kvg_ref[...]).astype(BF16)
    kv = _dot(kvn, wkv_ref[...])
    kr = p[:, o2:o2 + HEAD_SLAB] * cos + p[:, o2 + HEAD_SLAB:o2 + 2 * HEAD_SLAB] * sin
    for h in range(ATT_HEADS):
        lo = h * HEAD_SLAB
        k_ref[0, :, lo:lo + HEAD_SLAB] = (kv[:, lo:lo + HEAD_SLAB] + kr).astype(BF16)
    v_ref[0] = (kv[:, hw:] + vone_ref[...]).astype(BF16)

    o3 = o2 + 2 * HEAD_SLAB
    bg = p[:, o3:o3 + d_conv]
    z = p[:, o3 + d_conv:o3 + 2 * d_conv] * p[:, o3 + 2 * d_conv:o3 + 3 * d_conv]
    zbuf[8:8 + tm, :] = z
    conv = (cb_ref[...] + cw_ref[0:1, :] * zbuf[6:6 + tm, :] + cw_ref[1:2, :] * zbuf[7:7 + tm, :]
            + cw_ref[2:3, :] * z)
    cn_ref[0] = _rms(bg * conv, cg_ref[...]).astype(BF16)
    tail = zbuf[tm:tm + 8, :]
    zt_ref[0] = tail
    zbuf[0:8, :] = tail


def _proj_call(x, zprev, w, cos, sin, tm):
    b, s, d = x.shape
    d_conv = w['conv_b'].shape[-1]
    hw = ATT_HEADS * HEAD_SLAB
    nt = s // tm
    q_scale = float((QK_NOPE + QK_ROPE) ** -0.5 * LOG2E)
    const = lambda bi, ti: (0, 0)
    kern = functools.partial(_proj_kernel, tm=tm, d_conv=d_conv, q_scale=q_scale)
    return pl.pallas_call(
        kern,
        name="proj",
        grid=(b, nt),
        in_specs=[
            pl.BlockSpec((1, tm, d), lambda bi, ti: (bi, ti, 0)),
            pl.BlockSpec((8, d_conv), const),
            pl.BlockSpec((1, d), const),
            pl.BlockSpec(w['w_in'].shape, const),
            pl.BlockSpec((1, Q_LORA), const),
            pl.BlockSpec(w['w_q'].shape, const),
            pl.BlockSpec((1, KV_LORA), const),
            pl.BlockSpec(w['w_kv'].shape, const),
            pl.BlockSpec((CONV_WIDTH, d_conv), const),
            pl.BlockSpec((1, d_conv), const),
            pl.BlockSpec((1, d_conv), const),
            pl.BlockSpec((tm, HEAD_SLAB), lambda bi, ti: (ti, 0)),
            pl.BlockSpec((tm, HEAD_SLAB), lambda bi, ti: (ti, 0)),
            pl.BlockSpec((1, hw), const),
        ],
        out_specs=[
            pl.BlockSpec((1, tm, hw), lambda bi, ti: (bi, ti, 0)),
            pl.BlockSpec((1, tm, hw), lambda bi, ti: (bi, ti, 0)),
            pl.BlockSpec((1, tm, hw), lambda bi, ti: (bi, ti, 0)),
            pl.BlockSpec((1, tm, d_conv), lambda bi, ti: (bi, ti, 0)),
            pl.BlockSpec((1, 8, d_conv), lambda bi, ti: (bi, ti, 0)),
        ],
        out_shape=[
            jax.ShapeDtypeStruct((b, s, hw), BF16),
            jax.ShapeDtypeStruct((b, s, hw), BF16),
            jax.ShapeDtypeStruct((b, s, hw), BF16),
            jax.ShapeDtypeStruct((b, s, d_conv), BF16),
            jax.ShapeDtypeStruct((b, nt * 8, d_conv), F32),
        ],
        scratch_shapes=[pltpu.VMEM((tm + 8, d_conv), F32)],
        compiler_params=pltpu.CompilerParams(
            dimension_semantics=("arbitrary", "arbitrary"), vmem_limit_bytes=VMEM_LIMIT),
    )(x, zprev, w['norm1_g'], w['w_in'], w['q_norm_g'], w['w_q'], w['kv_norm_g'], w['w_kv'],
      w['conv_w'], w['conv_b'], w['conv_out_g'], cos, sin, w['v_one'])


def _v_lane(h):
    return V_HEAD * (h % 2)


def _one_lane(h):
    return V_HEAD * ((h + 1) % 2)


def _attn_kernel(q_ref, k_ref, v_ref, km_ref, vm_ref, o_ref, *, tq, tk, hg):
    i = pl.program_id(2)
    nd = tq // tk
    row = lax.broadcasted_iota(jnp.int32, (tq, tk), 0)
    col = lax.broadcasted_iota(jnp.int32, (tq, tk), 1)
    meta_ok = lax.broadcasted_iota(jnp.int32, (tq, LANES), 1) < N_META

    def step(carry, qh, kc, vc, mask):
        m, acc = carry
        s = _dot_nt(qh, kc)
        if mask is not None:
            s = jnp.where(mask, s, NEG_BIG)
        m_new = jnp.maximum(m, jnp.max(s, axis=-1, keepdims=True))
        pr = jnp.exp2(s - m_new)
        acc = jnp.exp2(m - m_new) * acc + _dot(pr.astype(BF16), vc)
        return m_new, acc

    slabs = [slice(hh * HEAD_SLAB, (hh + 1) * HEAD_SLAB) for hh in range(hg)]
    qs = [q_ref[0, :, sl] for sl in slabs]
    init = (jnp.full((tq, 1), NEG_BIG, F32), jnp.zeros((tq, LANES), F32))
    carries = tuple(step(init, qs[hh], km_ref[:, slabs[hh]], vm_ref[:, slabs[hh]], meta_ok)
                    for hh in range(hg))

    def body(j, cs):
        rows = pl.ds(pl.multiple_of(j * tk, tk), tk)
        return tuple(step(cs[hh], qs[hh], k_ref[0, rows, slabs[hh]], v_ref[0, rows, slabs[hh]], None)
                     for hh in range(hg))

    carries = lax.fori_loop(0, i * nd, body, carries)
    for c in range(nd):
        rows = pl.ds(pl.multiple_of(i * tq + c * tk, tk), tk)
        causal = col + c * tk <= row
        carries = tuple(step(carries[hh], qs[hh], k_ref[0, rows, slabs[hh]], v_ref[0, rows, slabs[hh]],
                             causal) for hh in range(hg))
    lane = lax.broadcasted_iota(jnp.int32, (tq, LANES), 1)
    outs = []
    for hh in range(hg):
        one = _one_lane(hh)
        acc = carries[hh][1]
        outs.append(jnp.where(lane == one, 0.0, acc / acc[:, one:one + 1]))
    o_ref[0] = jnp.concatenate([outs[2 * p] + outs[2 * p + 1] for p in range(hg // 2)], axis=-1).astype(BF16)


def _attn_call(q, k, v, km, vm, tq, hg):
    b, s, hw = q.shape
    gw = hg * HEAD_SLAB
    ow = (hg // 2) * LANES
    tk = min(TK_ATT, tq)
    kern = functools.partial(_attn_kernel, tq=tq, tk=tk, hg=hg)
    return pl.pallas_call(
        kern,
        name="attn",
        grid=(b, ATT_HEADS // hg, s // tq),
        in_specs=[
            pl.BlockSpec((1, tq, gw), lambda bi, gi, qi: (bi, qi, gi)),
            pl.BlockSpec((1, s, gw), lambda bi, gi, qi: (bi, 0, gi)),
            pl.BlockSpec((1, s, gw), lambda bi, gi, qi: (bi, 0, gi)),
            pl.BlockSpec((LANES, gw), lambda bi, gi, qi: (0, gi)),
            pl.BlockSpec((LANES, gw), lambda bi, gi, qi: (0, gi)),
        ],
        out_specs=pl.BlockSpec((1, tq, ow), lambda bi, gi, qi: (bi, qi, gi)),
        out_shape=jax.ShapeDtypeStruct((b, s, D_ATT), BF16),
        compiler_params=pltpu.CompilerParams(
            dimension_semantics=("arbitrary", "arbitrary", "arbitrary"), vmem_limit_bytes=VMEM_LIMIT),
    )(q, k, v, km, vm)


def _rank16_by_extraction(s):
    key = lax.broadcasted_iota(jnp.int32, s.shape, 0)
    rank = jnp.full(s.shape, float(PEER_TOPK), F32)
    for r in range(PEER_TOPK):
        m = jnp.max(s, axis=0, keepdims=True)
        first = jnp.min(jnp.where(s == m, key, N_KEYS), axis=0, keepdims=True)
        hit = key == first
        rank = jnp.where(hit, float(r), rank)
        s = jnp.where(hit, -jnp.inf, s)
    return rank


def _oddeven_merge(lo, hi, r):
    step = r * 2
    if step < hi - lo:
        yield from _oddeven_merge(lo, hi, step)
        yield from _oddeven_merge(lo + r, hi, step)
        yield from [(i, i + r) for i in range(lo + r, hi - r, step)]
    else:
        yield (lo, lo + r)


def _oddeven_merge_sort(lo, hi):
    if hi - lo >= 1:
        mid = lo + (hi - lo) // 2
        yield from _oddeven_merge_sort(lo, mid)
        yield from _oddeven_merge_sort(mid + 1, hi)
        yield from _oddeven_merge(lo, hi, 1)


_SORT16_NET = tuple(_oddeven_merge_sort(0, PEER_TOPK - 1))
SUBLANES = 8


def _sorted_top16(tiles):
    cols = list(tiles) + [None] * (PEER_TOPK - len(tiles))

    def vmax(x, y):
        return y if x is None else x if y is None else jnp.maximum(x, y)

    def vmin(x, y):
        return None if x is None or y is None else jnp.minimum(x, y)

    def exchange(i, j):
        cols[i], cols[j] = vmax(cols[i], cols[j]), vmin(cols[i], cols[j])

    for i, j in _SORT16_NET:
        exchange(i, j)
    for shift in (4, 2, 1):
        partner = [None if c is None else pltpu.roll(c, shift, axis=0) for c in reversed(cols)]
        cols = [vmax(cols[r], partner[r]) for r in range(PEER_TOPK)]
        for stride in (8, 4, 2, 1):
            for r in range(PEER_TOPK):
                if r & stride == 0:
                    exchange(r, r + stride)
    return cols


def _top16_tile(s):
    nv = N_KEYS // SUBLANES
    rows = [s[SUBLANES * v:SUBLANES * (v + 1), :] for v in range(nv)]
    top = _sorted_top16(rows)
    vals = jnp.concatenate([t[0:1, :] for t in top], axis=0)
    at_least = jnp.zeros((SUBLANES, LANES), F32)
    for v in range(nv):
        at_least = at_least + jnp.where(rows[v] >= top[PEER_TOPK - 1], 1.0, 0.0)
    tied = jnp.where(jnp.sum(at_least, axis=0, keepdims=True) != float(PEER_TOPK), 1.0, 0.0)
    for r in range(PEER_TOPK - 1):
        tied = tied + jnp.where(top[r][0:1, :] <= top[r + 1][0:1, :], 1.0, 0.0)
    rank = []
    for v in range(nv):
        rv = jnp.zeros((SUBLANES, LANES), F32)
        for r in range(PEER_TOPK):
            rv = rv + jnp.where(top[r] > rows[v], 1.0, 0.0)
        rank.append(rv)
    return vals, jnp.concatenate(rank, axis=0), tied


def _candidate_blocks():
    blocks = []
    for a in range(8):
        nb = PEER_TOPK // (a + 1)
        for b0 in range(0, nb, 8):
            nv = min(8, nb - b0)
            flat = np.array([a * PEER_TOPK + b0 + r for r in range(8)])
            blocks.append(('row', a, b0, nv, flat))
    flat = np.array([(8 + r) * PEER_TOPK for r in range(8)])
    blocks.append(('col', 8, 0, 8, flat))
    return blocks


_CAND_BLOCKS = _candidate_blocks()


def _select_counts(v1, v2, exact):
    sub = lax.broadcasted_iota(jnp.int32, (8, LANES), 0)
    vals = []
    for kind, a0, b0, nv, _ in _CAND_BLOCKS:
        if kind == 'row':
            blk = v1[a0:a0 + 1, :] + v2[b0:b0 + 8, :]
        else:
            blk = v1[a0:a0 + 8, :] + v2[b0:b0 + 1, :]
        if nv < 8:
            blk = jnp.where(sub < nv, blk, -jnp.inf)
        vals.append(blk)
    def by_pairwise_rank(blocks):
        beaten = [jnp.zeros((8, LANES), F32) for _ in blocks]
        for si, (_, _, _, nv_s, flat_s) in enumerate(_CAND_BLOCKS):
            for r in range(nv_s):
                src = jnp.broadcast_to(blocks[si][r:r + 1, :], (8, LANES))
                for ti, (_, _, _, _, flat_t) in enumerate(_CAND_BLOCKS):
                    wins_tie = flat_s[r] < flat_t
                    gt = jnp.where(src > blocks[ti], 1.0, 0.0)
                    ge = jnp.where(src >= blocks[ti], 1.0, 0.0)
                    if wins_tie.all():
                        beats = ge
                    elif not wins_tie.any():
                        beats = gt
                    else:
                        beats = jnp.where(sub >= int((~wins_tie).sum()), ge, gt)
                    beaten[ti] = beaten[ti] + beats
        return tuple(jnp.where(b < float(PEER_TOPK), 1.0, 0.0) for b in beaten)

    if exact:
        selected = by_pairwise_rank(vals)
        tied = jnp.zeros((1, LANES), F32)
    else:
        kth = _sorted_top16(vals)[PEER_TOPK - 1]
        selected = tuple(jnp.where(blk >= kth, 1.0, 0.0) for blk in vals)
        n_reach = jnp.sum(sum(selected[1:], selected[0]), axis=0, keepdims=True)
        tied = jnp.where(n_reach != float(PEER_TOPK), 1.0, 0.0)

    top = vals[0][0:1, :]
    z = jnp.zeros((1, LANES), F32)
    cnt_rows = [jnp.zeros((1, LANES), F32) for _ in range(PEER_TOPK)]
    for ti, (kind, a0, b0, nv, _) in enumerate(_CAND_BLOCKS):
        self_f = selected[ti]
        if nv < 8:
            self_f = jnp.where(sub < nv, self_f, 0.0)
        z = z + jnp.sum(self_f * jnp.exp(vals[ti] - top), axis=0, keepdims=True)
        if kind == 'row':
            cnt_rows[a0] = cnt_rows[a0] + jnp.sum(self_f, axis=0, keepdims=True)
        else:
            for r in range(8):
                cnt_rows[a0 + r] = cnt_rows[a0 + r] + self_f[r:r + 1, :]
    return jnp.concatenate(cnt_rows, axis=0), z, tied


def _mix_kernel(att_ref, cn_ref, x_ref, ag_ref, wo_ref, g2_ref, wqr_ref, k1_ref, k2_ref,
                h1_ref, xnt_ref, c_ref, e1_ref, r2_ref, e2_ref, st, rk, vs, *, tm):
    nlt = tm // LANES
    att = _rms(att_ref[...].astype(F32), ag_ref[...]).astype(BF16)
    d_att = att.shape[-1]
    h1 = x_ref[...] + _dot(att, wo_ref[0:d_att, :]) + _dot(cn_ref[...], wo_ref[d_att:, :])
    h1_ref[...] = h1
    xn32 = _rms(h1, g2_ref[...])
    xn = xn32.astype(BF16)
    xnt_ref[...] = xn32.T.astype(BF16)
    qp = _dot(xn, wqr_ref[...]).astype(BF16)
    half = D_KEY // 2
    for h in range(PEER_HEADS):
        for side, kref in enumerate((k1_ref, k2_ref)):
            lo = h * D_KEY + side * half
            sc = _dot_nt(kref[...], qp[:, lo:lo + half])
            for lt in range(nlt):
                st[2 * h + side, lt] = sc[:, lt * LANES:(lt + 1) * LANES]

    no_tie = jnp.zeros((1, LANES), F32)

    def rank_body(idx, tied):
        p = idx // nlt
        lt = idx % nlt
        vals, rank, t = _top16_tile(st[p, lt])
        rk[p, lt] = rank
        vs[p, lt] = vals
        return jnp.maximum(tied, t)

    def rank_exact_body(idx, carry):
        p = idx // nlt
        lt = idx % nlt
        rk[p, lt] = _rank16_by_extraction(st[p, lt])
        return carry

    tied = lax.fori_loop(0, 2 * PEER_HEADS * nlt, rank_body, no_tie)

    @pl.when(jnp.max(tied) > 0.0)
    def _():
        lax.fori_loop(0, 2 * PEER_HEADS * nlt, rank_exact_body, 0)

    def sel_body(exact, idx, tied):
        h = idx // (nlt // 2)
        lt2 = idx % (nlt // 2)
        for half in range(2):
            lt = 2 * lt2 + half
            ls = slice(half * LANES, (half + 1) * LANES)
            v1 = vs[2 * h, lt]
            v2 = vs[2 * h + 1, lt]
            cnt, z, t = _select_counts(v1, v2, exact)
            tied = jnp.maximum(tied, t)
            r1 = rk[2 * h, lt]
            c = jnp.zeros((N_KEYS, LANES), F32)
            for a in range(PEER_TOPK):
                c = c + jnp.where(r1 == float(a), cnt[a:a + 1, :], 0.0)
            c_ref[h, lt] = c
            e1_ref[h, lt] = jnp.exp(st[2 * h, lt] - v1[0:1, :]) * (1.0 / z)
            r2_ref[h, lt2, :, ls] = rk[2 * h + 1, lt].astype(BF16)
            e2_ref[h, lt2, :, ls] = jnp.exp(st[2 * h + 1, lt] - v2[0:1, :]).astype(BF16)
        return tied

    tied = lax.fori_loop(0, PEER_HEADS * (nlt // 2), functools.partial(sel_body, False), no_tie)

    @pl.when(jnp.max(tied) > 0.0)
    def _():
        lax.fori_loop(0, PEER_HEADS * (nlt // 2), functools.partial(sel_body, True), no_tie)


def _mix_call(att, cn, x2, w, tm):
    n, d = x2.shape
    nlt = tm // LANES
    d_mix = w['w_out'].shape[0]
    const = lambda ti: (0, 0)
    tile4 = pl.BlockSpec((PEER_HEADS, nlt, N_KEYS, LANES), lambda ti: (0, ti, 0, 0))
    tile4p = pl.BlockSpec((PEER_HEADS, nlt // 2, N_KEYS, 2 * LANES), lambda ti: (0, ti, 0, 0))
    sel_shape = (PEER_HEADS, n // LANES, N_KEYS, LANES)
    selp_shape = (PEER_HEADS, n // (2 * LANES), N_KEYS, 2 * LANES)
    kern = functools.partial(_mix_kernel, tm=tm)
    return pl.pallas_call(
        kern,
        name="mix",
        grid=(n // tm,),
        in_specs=[
            pl.BlockSpec((tm, att.shape[-1]), lambda ti: (ti, 0)),
            pl.BlockSpec((tm, cn.shape[-1]), lambda ti: (ti, 0)),
            pl.BlockSpec((tm, d), lambda ti: (ti, 0)),
            pl.BlockSpec((1, att.shape[-1]), const),
            pl.BlockSpec((d_mix, d), const),
            pl.BlockSpec((1, d), const),
            pl.BlockSpec(w['w_query'].shape, const),
            pl.BlockSpec((N_KEYS, D_KEY // 2), const),
            pl.BlockSpec((N_KEYS, D_KEY // 2), const),
        ],
        out_specs=[
            pl.BlockSpec((tm, d), lambda ti: (ti, 0)),
            pl.BlockSpec((d, tm), lambda ti: (0, ti)),
            tile4, tile4, tile4p, tile4p,
        ],
        out_shape=[
            jax.ShapeDtypeStruct((n, d), F32),
            jax.ShapeDtypeStruct((d, n), BF16),
            jax.ShapeDtypeStruct(sel_shape, F32),
            jax.ShapeDtypeStruct(sel_shape, F32),
            jax.ShapeDtypeStruct(selp_shape, BF16),
            jax.ShapeDtypeStruct(selp_shape, BF16),
        ],
        scratch_shapes=[
            pltpu.VMEM((2 * PEER_HEADS, nlt, N_KEYS, LANES), F32),
            pltpu.VMEM((2 * PEER_HEADS, nlt, N_KEYS, LANES), F32),
            pltpu.VMEM((2 * PEER_HEADS, nlt, PEER_TOPK, LANES), F32),
        ],
        compiler_params=pltpu.CompilerParams(
            dimension_semantics=("arbitrary",), vmem_limit_bytes=VMEM_LIMIT),
    )(att, cn, x2, w['attn_out_g'], w['w_out'], w['norm2_g'], w['w_query'], w['keys1'], w['keys2'])


def _gelu_tanh(x):
    c = float(np.sqrt(2.0 / np.pi))
    hx = 0.5 * x
    return hx + hx * jnp.tanh(x * (c + (c * 0.044715) * (x * x)))


PACK_ROWS = 16


def _peer_kernel(xnt_ref, u_ref, vt_ref, c_ref, e1_ref, r2_ref, e2_ref, h1_ref, fg_ref, y_ref,
                 acc, pbuf, tbl, *, t, ec):
    e = pl.program_id(1)
    nlt = t // LANES
    nib = ec // N_KEYS
    nsub = ec // SUB_EXPERTS
    kps = SUB_EXPERTS // N_KEYS
    npos = N_KEYS // PACK_ROWS

    @pl.when(e == 0)
    def _():
        acc[...] = jnp.zeros_like(acc)

    row0 = pl.multiple_of(e * nib, nib)
    for h in range(PEER_HEADS):
        for lt in range(nlt):
            ls = slice((lt % 2) * LANES, (lt % 2 + 1) * LANES)
            c8 = c_ref[h, lt, pl.ds(row0, nib), :]
            e8 = e1_ref[h, lt, pl.ds(row0, nib), :]
            for kk in range(nib):
                tbl[0, h, lt // 2, kk, :, ls] = jnp.broadcast_to(c8[kk:kk + 1, :], (PACK_ROWS, LANES)).astype(BF16)
                tbl[1, h, lt // 2, kk, :, ls] = jnp.broadcast_to(e8[kk:kk + 1, :], (PACK_ROWS, LANES)).astype(BF16)

    for sb in range(nsub):
        act = _dot(u_ref[sb * SUB_EXPERTS:(sb + 1) * SUB_EXPERTS, :], xnt_ref[...])
        for k in range(kps):
            for lt2 in range(nlt // 2):
                ls = slice(lt2 * 2 * LANES, (lt2 + 1) * 2 * LANES)
                gate = None
                for h in range(PEER_HEADS):
                    cb = jnp.concatenate([tbl[0, h, lt2, sb * kps + k]] * npos, axis=0)
                    eb = jnp.concatenate([tbl[1, h, lt2, sb * kps + k]] * npos, axis=0)
                    term = jnp.where(r2_ref[h, lt2] < cb, e2_ref[h, lt2] * eb, jnp.zeros((), BF16))
                    gate = term if gate is None else gate + term
                gl = _gelu_tanh(act[k * N_KEYS:(k + 1) * N_KEYS, ls]).astype(BF16)
                pbuf[sb, k * N_KEYS:(k + 1) * N_KEYS, ls] = gate * gl
        acc[...] += _dot(vt_ref[sb], pbuf[sb])

    @pl.when(e == pl.num_programs(1) - 1)
    def _():
        h2 = h1_ref[...] + acc[...].T
        y_ref[...] = _rms(h2, fg_ref[...])


def _peer_call(xnt, u, vt, c, e1, r2, e2, h1, fg, t, ec):
    d, n = xnt.shape
    ne = u.shape[0] // ec
    nsub = ec // SUB_EXPERTS
    assert vt.shape == (ne * nsub, d, SUB_EXPERTS)
    nlt = t // LANES
    tile4 = pl.BlockSpec((PEER_HEADS, nlt, N_KEYS, LANES), lambda ti, ei: (0, ti, 0, 0))
    tile4p = pl.BlockSpec((PEER_HEADS, nlt // 2, N_KEYS, 2 * LANES), lambda ti, ei: (0, ti, 0, 0))
    kern = functools.partial(_peer_kernel, t=t, ec=ec)
    return pl.pallas_call(
        kern,
        name="peer",
        grid=(n // t, ne),
        in_specs=[
            pl.BlockSpec((d, t), lambda ti, ei: (0, ti)),
            pl.BlockSpec((ec, d), lambda ti, ei: (ei, 0)),
            pl.BlockSpec((nsub, d, SUB_EXPERTS), lambda ti, ei: (ei, 0, 0)),
            tile4, tile4, tile4p, tile4p,
            pl.BlockSpec((t, d), lambda ti, ei: (ti, 0)),
            pl.BlockSpec((1, d), lambda ti, ei: (0, 0)),
        ],
        out_specs=pl.BlockSpec((t, d), lambda ti, ei: (ti, 0)),
        out_shape=jax.ShapeDtypeStruct((n, d), F32),
        scratch_shapes=[pltpu.VMEM((d, t), F32), pltpu.VMEM((nsub, SUB_EXPERTS, t), BF16),
                        pltpu.VMEM((2, PEER_HEADS, nlt // 2, ec // N_KEYS, PACK_ROWS, 2 * LANES), BF16)],
        compiler_params=pltpu.CompilerParams(
            dimension_semantics=("arbitrary", "arbitrary"), vmem_limit_bytes=VMEM_LIMIT),
    )(xnt, u, vt, c, e1, r2, e2, h1, fg)


def _rope_slab_tables(length):
    inv = 1.0 / (ROPE_THETA ** (jnp.arange(0, QK_ROPE, 2, dtype=F32) / QK_ROPE))
    ang = jnp.arange(length, dtype=F32)[:, None] * inv[None, :]
    cos, sin = jnp.cos(ang), jnp.sin(ang)
    half = QK_ROPE // 2
    pad = HEAD_SLAB - QK_NOPE - QK_ROPE
    ones = jnp.ones((length, QK_NOPE), F32)
    zeros_n = jnp.zeros((length, QK_NOPE), F32)
    zeros_p = jnp.zeros((length, pad), F32)
    del half
    cos_t = jnp.concatenate([ones, cos, cos, zeros_p], axis=1)
    sin_t = jnp.concatenate([zeros_n, -sin, sin, zeros_p], axis=1)
    return cos_t, sin_t


def _head_slabs(wcols, n_heads, width, offset=0):
    k = wcols.shape[0]
    w3 = wcols.reshape(k, n_heads, width)

    def at(off):
        return jnp.pad(w3, ((0, 0), (0, 0), (off, HEAD_SLAB - width - off)))

    if isinstance(offset, int):
        return at(offset).reshape(k, n_heads * HEAD_SLAB)
    out = None
    for off in sorted(set(offset)):
        pick = jnp.asarray([o == off for o in offset])[None, :, None]
        out = jnp.where(pick, at(off), 0.0) if out is None else jnp.where(pick, at(off), out)
    return out.reshape(k, n_heads * HEAD_SLAB)


def _prep_weights(norm1_g, w_in, q_norm_g, w_uq, kv_norm_g, w_ukv, conv_w, conv_b, attn_out_g,
                  conv_out_g, w_out, norm2_g, peer_w_query, peer_keys1, peer_keys2):
    half = QK_ROPE // 2
    o1 = Q_LORA
    o2 = o1 + KV_LORA
    o3 = o2 + QK_ROPE
    k_in = w_in.shape[0]
    kr = w_in[:, o2:o3]
    zpad_n = jnp.zeros((k_in, QK_NOPE), F32)
    zpad_p = jnp.zeros((k_in, HEAD_SLAB - QK_NOPE - QK_ROPE), F32)
    kr_a = jnp.concatenate([zpad_n, kr[:, :half], kr[:, half:], zpad_p], axis=1)
    kr_b = jnp.concatenate([zpad_n, kr[:, half:], kr[:, :half], zpad_p], axis=1)
    w_in_ext = jnp.concatenate([w_in[:, :o2], kr_a, kr_b, w_in[:, o3:]], axis=1).astype(BF16)

    qd = QK_NOPE + QK_ROPE
    wq3 = w_uq.reshape(Q_LORA, ATT_HEADS, qd)
    wq_swap = jnp.concatenate([wq3[:, :, :QK_NOPE], wq3[:, :, QK_NOPE + half:], wq3[:, :, QK_NOPE:QK_NOPE + half]],
                              axis=2).reshape(Q_LORA, ATT_HEADS * qd)
    w_q = jnp.concatenate([_head_slabs(w_uq, ATT_HEADS, qd), _head_slabs(wq_swap, ATT_HEADS, qd)],
                          axis=1).astype(BF16)

    wkv3 = w_ukv.reshape(KV_LORA, ATT_HEADS, QK_NOPE + V_HEAD)
    wk = wkv3[:, :, :QK_NOPE].reshape(KV_LORA, ATT_HEADS * QK_NOPE)
    wv = wkv3[:, :, QK_NOPE:].reshape(KV_LORA, ATT_HEADS * V_HEAD)
    v_off = [_v_lane(h) for h in range(ATT_HEADS)]
    w_kv = jnp.concatenate([_head_slabs(wk, ATT_HEADS, QK_NOPE), _head_slabs(wv, ATT_HEADS, V_HEAD, v_off)],
                           axis=1).astype(BF16)
    one_cols = np.zeros((1, ATT_HEADS * HEAD_SLAB), np.float32)
    for h in range(ATT_HEADS):
        one_cols[0, h * HEAD_SLAB + _one_lane(h)] = 1.0
    return {
        'v_one': jnp.asarray(one_cols),
        'norm1_g': norm1_g.reshape(1, -1), 'w_in': w_in_ext,
        'q_norm_g': q_norm_g.reshape(1, -1), 'w_q': w_q,
        'kv_norm_g': kv_norm_g.reshape(1, -1), 'w_kv': w_kv,
        'conv_w': conv_w, 'conv_b': conv_b.reshape(1, -1), 'conv_out_g': conv_out_g.reshape(1, -1),
        'attn_out_g': attn_out_g.reshape(1, -1), 'w_out': w_out.astype(BF16),
        'norm2_g': norm2_g.reshape(1, -1), 'w_query': peer_w_query.astype(BF16),
        'keys1': peer_keys1.astype(BF16), 'keys2': peer_keys2.astype(BF16),
    }


def kernel(x, meta_tokens, norm1_g, w_in, q_norm_g, w_uq, kv_norm_g, w_ukv, conv_w, conv_b, attn_out_g,
           conv_out_g, w_out, norm2_g, peer_w_query, peer_keys1, peer_keys2, peer_u, peer_v, final_g):
    b, s, d = x.shape
    assert norm1_g.shape[0] == 1, "single-layer kernel"
    assert meta_tokens.shape[0] == N_META
    tm = min(TM_PROJ, s)
    assert s % tm == 0 and s % LANES == 0
    w = _prep_weights(norm1_g[0], w_in[0], q_norm_g[0], w_uq[0], kv_norm_g[0], w_ukv[0], conv_w[0],
                      conv_b[0], attn_out_g[0], conv_out_g[0], w_out[0], norm2_g[0], peer_w_query[0],
                      peer_keys1[0], peer_keys2[0])
    d_conv = conv_b.shape[-1]
    cos_t, sin_t = _rope_slab_tables(N_META + s)

    _, k_m, v_m, _, z_m = _proj_call(meta_tokens[None], jnp.zeros((8, d_conv), F32), w,
                                     cos_t[:N_META], sin_t[:N_META], N_META)
    q, k, v, cn, _ = _proj_call(x, z_m[0], w, cos_t[N_META:], sin_t[N_META:], tm)
    hw = ATT_HEADS * HEAD_SLAB
    km = jnp.pad(k_m[0], ((0, LANES - N_META), (0, 0)))
    vm = jnp.pad(v_m[0], ((0, LANES - N_META), (0, 0)))
    att = _attn_call(q, k, v, km, vm, min(TQ_ATT, s), HEADS_PER_STEP)

    n = b * s
    tmix = min(TM_MIX, n)
    h1, xnt, c, e1, r2, e2 = _mix_call(att.reshape(n, D_ATT), cn.reshape(n, d_conv), x.reshape(n, d), w, tmix)
    u_bf = peer_u[0].astype(BF16)
    vt_bf = peer_v[0].reshape(-1, SUB_EXPERTS, d).transpose(0, 2, 1).astype(BF16)
    y = _peer_call(xnt, u_bf, vt_bf, c, e1, r2, e2, h1, final_g.reshape(1, -1), min(T_PEER, n), E_CHUNK)
    return y.reshape(b, s, d)
```

```python
import functools

import numpy as np
import jax
import jax.numpy as jnp
from jax import lax
from jax.experimental import pallas as pl
from jax.experimental.pallas import tpu as pltpu

F32 = jnp.float32
BF16 = jnp.bfloat16

N_META = 16
EPS = 1e-6
ROPE_THETA = 10000.0
ATT_HEADS = 8
Q_LORA = 256
KV_LORA = 128
QK_NOPE = 64
QK_ROPE = 32
V_HEAD = 64
D_ATT = ATT_HEADS * V_HEAD
PEER_HEADS = 8
N_KEYS = 128
D_KEY = 256
PEER_TOPK = 16
CONV_WIDTH = 3

LANES = 128
HEAD_SLAB = LANES
NEG_BIG = -1e30
LOG2E = 1.4426950408889634
VMEM_LIMIT = 56 * 1024 * 1024

TM_PROJ = 512
TQ_ATT = 512
TK_ATT = 512
HEADS_PER_STEP = 4
TM_MIX = 512
T_PEER = 512
E_CHUNK = 2048
SUB_EXPERTS = 2048


def _dot(a, b):
    return jnp.dot(a, b, preferred_element_type=F32)


def _dot_nt(a, b):
    return lax.dot_general(a, b, (((1,), (1,)), ((), ())), preferred_element_type=F32)


def _rms(x, g):
    y = x * lax.rsqrt(jnp.mean(x * x, axis=-1, keepdims=True) + EPS)
    return y * g


def _proj_kernel(x_ref, zprev_ref, g1_ref, win_ref, qg_ref, wq_ref, kvg_ref, wkv_ref, cw_ref, cb_ref,
                 cg_ref, cos_ref, sin_ref, vone_ref,
                 q_ref, k_ref, v_ref, cn_ref, zt_ref, zbuf, *, tm, d_conv, q_scale):
    t = pl.program_id(1)

    @pl.when(t == 0)
    def _():
        zbuf[0:8, :] = zprev_ref[...]

    x = x_ref[0]
    hn = _rms(x, g1_ref[...]).astype(BF16)
    p = _dot(hn, win_ref[...])
    cos = cos_ref[...]
    sin = sin_ref[...]

    qn = _rms(p[:, :Q_LORA], qg_ref[...]).astype(BF16)
    qq = _dot(qn, wq_ref[...])
    hw = ATT_HEADS * HEAD_SLAB
    for h in range(ATT_HEADS):
        lo = h * HEAD_SLAB
        qh = qq[:, lo:lo + HEAD_SLAB] * cos + qq[:, hw + lo:hw + lo + HEAD_SLAB] * sin
        q_ref[0, :, lo:lo + HEAD_SLAB] = (qh * q_scale).astype(BF16)

    o1 = Q_LORA
    o2 = o1 + KV_LORA
    kvn = _rms(p[:, o1:o2], kvg_ref[...]).astype(BF16)
    kv = _dot(kvn, wkv_ref[...])
    kr = p[:, o2:o2 + HEAD_SLAB] * cos + p[:, o2 + HEAD_SLAB:o2 + 2 * HEAD_SLAB] * sin
    for h in range(ATT_HEADS):
        lo = h * HEAD_SLAB
        k_ref[0, :, lo:lo + HEAD_SLAB] = (kv[:, lo:lo + HEAD_SLAB] + kr).astype(BF16)
    v_ref[0] = (kv[:, hw:] + vone_ref[...]).astype(BF16)

    o3 = o2 + 2 * HEAD_SLAB
    bg = p[:, o3:o3 + d_conv]
    z = p[:, o3 + d_conv:o3 + 2 * d_conv] * p[:, o3 + 2 * d_conv:o3 + 3 * d_conv]
    zbuf[8:8 + tm, :] = z
    conv = (cb_ref[...] + cw_ref[0:1, :] * zbuf[6:6 + tm, :] + cw_ref[1:2, :] * zbuf[7:7 + tm, :]
            + cw_ref[2:3, :] * z)
    cn_ref[0] = _rms(bg * conv, cg_ref[...]).astype(BF16)
    tail = zbuf[tm:tm + 8, :]
    zt_ref[0] = tail
    zbuf[0:8, :] = tail


def _proj_call(x, zprev, w, cos, sin, tm):
    b, s, d = x.shape
    d_conv = w['conv_b'].shape[-1]
    hw = ATT_HEADS * HEAD_SLAB
    nt = s // tm
    q_scale = float((QK_NOPE + QK_ROPE) ** -0.5 * LOG2E)
    const = lambda bi, ti: (0, 0)
    kern = functools.partial(_proj_kernel, tm=tm, d_conv=d_conv, q_scale=q_scale)
    return pl.pallas_call(
        kern,
        name="proj",
        grid=(b, nt),
        in_specs=[
            pl.BlockSpec((1, tm, d), lambda bi, ti: (bi, ti, 0)),
            pl.BlockSpec((8, d_conv), const),
            pl.BlockSpec((1, d), const),
            pl.BlockSpec(w['w_in'].shape, const),
            pl.BlockSpec((1, Q_LORA), const),
            pl.BlockSpec(w['w_q'].shape, const),
            pl.BlockSpec((1, KV_LORA), const),
            pl.BlockSpec(w['w_kv'].shape, const),
            pl.BlockSpec((CONV_WIDTH, d_conv), const),
            pl.BlockSpec((1, d_conv), const),
            pl.BlockSpec((1, d_conv), const),
            pl.BlockSpec((tm, HEAD_SLAB), lambda bi, ti: (ti, 0)),
            pl.BlockSpec((tm, HEAD_SLAB), lambda bi, ti: (ti, 0)),
            pl.BlockSpec((1, hw), const),
        ],
        out_specs=[
            pl.BlockSpec((1, tm, hw), lambda bi, ti: (bi, ti, 0)),
            pl.BlockSpec((1, tm, hw), lambda bi, ti: (bi, ti, 0)),
            pl.BlockSpec((1, tm, hw), lambda bi, ti: (bi, ti, 0)),
            pl.BlockSpec((1, tm, d_conv), lambda bi, ti: (bi, ti, 0)),
            pl.BlockSpec((1, 8, d_conv), lambda bi, ti: (bi, ti, 0)),
        ],
        out_shape=[
            jax.ShapeDtypeStruct((b, s, hw), BF16),
            jax.ShapeDtypeStruct((b, s, hw), BF16),
            jax.ShapeDtypeStruct((b, s, hw), BF16),
            jax.ShapeDtypeStruct((b, s, d_conv), BF16),
            jax.ShapeDtypeStruct((b, nt * 8, d_conv), F32),
        ],
        scratch_shapes=[pltpu.VMEM((tm + 8, d_conv), F32)],
        compiler_params=pltpu.CompilerParams(
            dimension_semantics=("arbitrary", "arbitrary"), vmem_limit_bytes=VMEM_LIMIT),
    )(x, zprev, w['norm1_g'], w['w_in'], w['q_norm_g'], w['w_q'], w['kv_norm_g'], w['w_kv'],
      w['conv_w'], w['conv_b'], w['conv_out_g'], cos, sin, w['v_one'])


def _v_lane(h):
    return V_HEAD * (h % 2)


def _one_lane(h):
    return V_HEAD * ((h + 1) % 2)


def _attn_kernel(q_ref, k_ref, v_ref, km_ref, vm_ref, o_ref, *, tq, tk, hg):
    i = pl.program_id(2)
    nd = tq // tk
    row = lax.broadcasted_iota(jnp.int32, (tq, tk), 0)
    col = lax.broadcasted_iota(jnp.int32, (tq, tk), 1)
    meta_ok = lax.broadcasted_iota(jnp.int32, (tq, LANES), 1) < N_META

    def step(carry, qh, kc, vc, mask):
        m, acc = carry
        s = _dot_nt(qh, kc)
        if mask is not None:
            s = jnp.where(mask, s, NEG_BIG)
        m_new = jnp.maximum(m, jnp.max(s, axis=-1, keepdims=True))
        pr = jnp.exp2(s - m_new)
        acc = jnp.exp2(m - m_new) * acc + _dot(pr.astype(BF16), vc)
        return m_new, acc

    slabs = [slice(hh * HEAD_SLAB, (hh + 1) * HEAD_SLAB) for hh in range(hg)]
    qs = [q_ref[0, :, sl] for sl in slabs]
    init = (jnp.full((tq, 1), NEG_BIG, F32), jnp.zeros((tq, LANES), F32))
    carries = tuple(step(init, qs[hh], km_ref[:, slabs[hh]], vm_ref[:, slabs[hh]], meta_ok)
                    for hh in range(hg))

    def body(j, cs):
        rows = pl.ds(pl.multiple_of(j * tk, tk), tk)
        return tuple(step(cs[hh], qs[hh], k_ref[0, rows, slabs[hh]], v_ref[0, rows, slabs[hh]], None)
                     for hh in range(hg))

    carries = lax.fori_loop(0, i * nd, body, carries)
    for c in range(nd):
        rows = pl.ds(pl.multiple_of(i * tq + c * tk, tk), tk)
        causal = col + c * tk <= row
        carries = tuple(step(carries[hh], qs[hh], k_ref[0, rows, slabs[hh]], v_ref[0, rows, slabs[hh]],
                             causal) for hh in range(hg))
    lane = lax.broadcasted_iota(jnp.int32, (tq, LANES), 1)
    outs = []
    for hh in range(hg):
        one = _one_lane(hh)
        acc = carries[hh][1]
        outs.append(jnp.where(lane == one, 0.0, acc / acc[:, one:one + 1]))
    o_ref[0] = jnp.concatenate([outs[2 * p] + outs[2 * p + 1] for p in range(hg // 2)], axis=-1).astype(BF16)


def _attn_call(q, k, v, km, vm, tq, hg):
    b, s, hw = q.shape
    gw = hg * HEAD_SLAB
    ow = (hg // 2) * LANES
    tk = min(TK_ATT, tq)
    kern = functools.partial(_attn_kernel, tq=tq, tk=tk, hg=hg)
    return pl.pallas_call(
        kern,
        name="attn",
        grid=(b, ATT_HEADS // hg, s // tq),
        in_specs=[
            pl.BlockSpec((1, tq, gw), lambda bi, gi, qi: (bi, qi, gi)),
            pl.BlockSpec((1, s, gw), lambda bi, gi, qi: (bi, 0, gi)),
            pl.BlockSpec((1, s, gw), lambda bi, gi, qi: (bi, 0, gi)),
            pl.BlockSpec((LANES, gw), lambda bi, gi, qi: (0, gi)),
            pl.BlockSpec((LANES, gw), lambda bi, gi, qi: (0, gi)),
        ],
        out_specs=pl.BlockSpec((1, tq, ow), lambda bi, gi, qi: (bi, qi, gi)),
        out_shape=jax.ShapeDtypeStruct((b, s, D_ATT), BF16),
        compiler_params=pltpu.CompilerParams(
            dimension_semantics=("arbitrary", "arbitrary", "arbitrary"), vmem_limit_bytes=VMEM_LIMIT),
    )(q, k, v, km, vm)


def _rank16_by_extraction(s):
    key = lax.broadcasted_iota(jnp.int32, s.shape, 0)
    rank = jnp.full(s.shape, float(PEER_TOPK), F32)
    for r in range(PEER_TOPK):
        m = jnp.max(s, axis=0, keepdims=True)
        first = jnp.min(jnp.where(s == m, key, N_KEYS), axis=0, keepdims=True)
        hit = key == first
        rank = jnp.where(hit, float(r), rank)
        s = jnp.where(hit, -jnp.inf, s)
    return rank


def _oddeven_merge(lo, hi, r):
    step = r * 2
    if step < hi - lo:
        yield from _oddeven_merge(lo, hi, step)
        yield from _oddeven_merge(lo + r, hi, step)
        yield from [(i, i + r) for i in range(lo + r, hi - r, step)]
    else:
        yield (lo, lo + r)


def _oddeven_merge_sort(lo, hi):
    if hi - lo >= 1:
        mid = lo + (hi - lo) // 2
        yield from _oddeven_merge_sort(lo, mid)
        yield from _oddeven_merge_sort(mid + 1, hi)
        yield from _oddeven_merge(lo, hi, 1)


_SORT16_NET = tuple(_oddeven_merge_sort(0, PEER_TOPK - 1))
SUBLANES = 8


def _sorted_top16(tiles):
    cols = list(tiles) + [None] * (PEER_TOPK - len(tiles))

    def vmax(x, y):
        return y if x is None else x if y is None else jnp.maximum(x, y)

    def vmin(x, y):
        return None if x is None or y is None else jnp.minimum(x, y)

    def exchange(i, j):
        cols[i], cols[j] = vmax(cols[i], cols[j]), vmin(cols[i], cols[j])

    for i, j in _SORT16_NET:
        exchange(i, j)
    for shift in (4, 2, 1):
        partner = [None if c is None else pltpu.roll(c, shift, axis=0) for c in reversed(cols)]
        cols = [vmax(cols[r], partner[r]) for r in range(PEER_TOPK)]
        for stride in (8, 4, 2, 1):
            for r in range(PEER_TOPK):
                if r & stride == 0:
                    exchange(r, r + stride)
    return cols


def _top16_tile(s):
    nv = N_KEYS // SUBLANES
    rows = [s[SUBLANES * v:SUBLANES * (v + 1), :] for v in range(nv)]
    top = _sorted_top16(rows)
    vals = jnp.concatenate([t[0:1, :] for t in top], axis=0)
    at_least = jnp.zeros((SUBLANES, LANES), F32)
    for v in range(nv):
        at_least = at_least + jnp.where(rows[v] >= top[PEER_TOPK - 1], 1.0, 0.0)
    tied = jnp.where(jnp.sum(at_least, axis=0, keepdims=True) != float(PEER_TOPK), 1.0, 0.0)
    for r in range(PEER_TOPK - 1):
        tied = tied + jnp.where(top[r][0:1, :] <= top[r + 1][0:1, :], 1.0, 0.0)
    rank = []
    for v in range(nv):
        rv = jnp.zeros((SUBLANES, LANES), F32)
        for r in range(PEER_TOPK):
            rv = rv + jnp.where(top[r] > rows[v], 1.0, 0.0)
        rank.append(rv)
    return vals, jnp.concatenate(rank, axis=0), tied


def _candidate_blocks():
    blocks = []
    for a in range(8):
        nb = PEER_TOPK // (a + 1)
        for b0 in range(0, nb, 8):
            nv = min(8, nb - b0)
            flat = np.array([a * PEER_TOPK + b0 + r for r in range(8)])
            blocks.append(('row', a, b0, nv, flat))
    flat = np.array([(8 + r) * PEER_TOPK for r in range(8)])
    blocks.append(('col', 8, 0, 8, flat))
    return blocks


_CAND_BLOCKS = _candidate_blocks()


def _select_counts(v1, v2, exact):
    sub = lax.broadcasted_iota(jnp.int32, (8, LANES), 0)
    vals = []
    for kind, a0, b0, nv, _ in _CAND_BLOCKS:
        if kind == 'row':
            blk = v1[a0:a0 + 1, :] + v2[b0:b0 + 8, :]
        else:
            blk = v1[a0:a0 + 8, :] + v2[b0:b0 + 1, :]
        if nv < 8:
            blk = jnp.where(sub < nv, blk, -jnp.inf)
        vals.append(blk)
    def by_pairwise_rank(blocks):
        beaten = [jnp.zeros((8, LANES), F32) for _ in blocks]
        for si, (_, _, _, nv_s, flat_s) in enumerate(_CAND_BLOCKS):
            for r in range(nv_s):
                src = jnp.broadcast_to(blocks[si][r:r + 1, :], (8, LANES))
                for ti, (_, _, _, _, flat_t) in enumerate(_CAND_BLOCKS):
                    wins_tie = flat_s[r] < flat_t
                    gt = jnp.where(src > blocks[ti], 1.0, 0.0)
                    ge = jnp.where(src >= blocks[ti], 1.0, 0.0)
                    if wins_tie.all():
                        beats = ge
                    elif not wins_tie.any():
                        beats = gt
                    else:
                        beats = jnp.where(sub >= int((~wins_tie).sum()), ge, gt)
                    beaten[ti] = beaten[ti] + beats
        return tuple(jnp.where(b < float(PEER_TOPK), 1.0, 0.0) for b in beaten)

    if exact:
        selected = by_pairwise_rank(vals)
        tied = jnp.zeros((1, LANES), F32)
    else:
        kth = _sorted_top16(vals)[PEER_TOPK - 1]
        selected = tuple(jnp.where(blk >= kth, 1.0, 0.0) for blk in vals)
        n_reach = jnp.sum(sum(selected[1:], selected[0]), axis=0, keepdims=True)
        tied = jnp.where(n_reach != float(PEER_TOPK), 1.0, 0.0)

    top = vals[0][0:1, :]
    z = jnp.zeros((1, LANES), F32)
    cnt_rows = [jnp.zeros((1, LANES), F32) for _ in range(PEER_TOPK)]
    for ti, (kind, a0, b0, nv, _) in enumerate(_CAND_BLOCKS):
        self_f = selected[ti]
        if nv < 8:
            self_f = jnp.where(sub < nv, self_f, 0.0)
        z = z + jnp.sum(self_f * jnp.exp(vals[ti] - top), axis=0, keepdims=True)
        if kind == 'row':
            cnt_rows[a0] = cnt_rows[a0] + jnp.sum(self_f, axis=0, keepdims=True)
        else:
            for r in range(8):
                cnt_rows[a0 + r] = cnt_rows[a0 + r] + self_f[r:r + 1, :]
    return jnp.concatenate(cnt_rows, axis=0), z, tied


def _mix_kernel(att_ref, cn_ref, x_ref, ag_ref, wo_ref, g2_ref, wqr_ref, k1_ref, k2_ref,
                h1_ref, xnt_ref, c_ref, e1_ref, r2_ref, e2_ref, st, rk, vs, *, tm):
    nlt = tm // LANES
    att = _rms(att_ref[...].astype(F32), ag_ref[...]).astype(BF16)
    d_att = att.shape[-1]
    h1 = x_ref[...] + _dot(att, wo_ref[0:d_att, :]) + _dot(cn_ref[...], wo_ref[d_att:, :])
    h1_ref[...] = h1
    xn32 = _rms(h1, g2_ref[...])
    xn = xn32.astype(BF16)
    xnt_ref[...] = xn32.T.astype(BF16)
    qp = _dot(xn, wqr_ref[...]).astype(BF16)
    half = D_KEY // 2
    for h in range(PEER_HEADS):
        for side, kref in enumerate((k1_ref, k2_ref)):
            lo = h * D_KEY + side * half
            sc = _dot_nt(kref[...], qp[:, lo:lo + half])
            for lt in range(nlt):
                st[2 * h + side, lt] = sc[:, lt * LANES:(lt + 1) * LANES]

    no_tie = jnp.zeros((1, LANES), F32)

    def rank_body(idx, tied):
        p = idx // nlt
        lt = idx % nlt
        vals, rank, t = _top16_tile(st[p, lt])
        rk[p, lt] = rank
        vs[p, lt] = vals
        return jnp.maximum(tied, t)

    def rank_exact_body(idx, carry):
        p = idx // nlt
        lt = idx % nlt
        rk[p, lt] = _rank16_by_extraction(st[p, lt])
        return carry

    tied = lax.fori_loop(0, 2 * PEER_HEADS * nlt, rank_body, no_tie)

    @pl.when(jnp.max(tied) > 0.0)
    def _():
        lax.fori_loop(0, 2 * PEER_HEADS * nlt, rank_exact_body, 0)

    def sel_body(exact, idx, tied):
        h = idx // (nlt // 2)
        lt2 = idx % (nlt // 2)
        for half in range(2):
            lt = 2 * lt2 + half
            ls = slice(half * LANES, (half + 1) * LANES)
            v1 = vs[2 * h, lt]
            v2 = vs[2 * h + 1, lt]
            cnt, z, t = _select_counts(v1, v2, exact)
            tied = jnp.maximum(tied, t)
            r1 = rk[2 * h, lt]
            c = jnp.zeros((N_KEYS, LANES), F32)
            for a in range(PEER_TOPK):
                c = c + jnp.where(r1 == float(a), cnt[a:a + 1, :], 0.0)
            c_ref[h, lt] = c
            e1_ref[h, lt] = jnp.exp(st[2 * h, lt] - v1[0:1, :]) * (1.0 / z)
            r2_ref[h, lt2, :, ls] = rk[2 * h + 1, lt].astype(BF16)
            e2_ref[h, lt2, :, ls] = jnp.exp(st[2 * h + 1, lt] - v2[0:1, :]).astype(BF16)
        return tied

    tied = lax.fori_loop(0, PEER_HEADS * (nlt // 2), functools.partial(sel_body, False), no_tie)

    @pl.when(jnp.max(tied) > 0.0)
    def _():
        lax.fori_loop(0, PEER_HEADS * (nlt // 2), functools.partial(sel_body, True), no_tie)


def _mix_call(att, cn, x2, w, tm):
    n, d = x2.shape
    nlt = tm // LANES
    d_mix = w['w_out'].shape[0]
    const = lambda ti: (0, 0)
    tile4 = pl.BlockSpec((PEER_HEADS, nlt, N_KEYS, LANES), lambda ti: (0, ti, 0, 0))
    tile4p = pl.BlockSpec((PEER_HEADS, nlt // 2, N_KEYS, 2 * LANES), lambda ti: (0, ti, 0, 0))
    sel_shape = (PEER_HEADS, n // LANES, N_KEYS, LANES)
    selp_shape = (PEER_HEADS, n // (2 * LANES), N_KEYS, 2 * LANES)
    kern = functools.partial(_mix_kernel, tm=tm)
    return pl.pallas_call(
        kern,
        name="mix",
        grid=(n // tm,),
        in_specs=[
            pl.BlockSpec((tm, att.shape[-1]), lambda ti: (ti, 0)),
            pl.BlockSpec((tm, cn.shape[-1]), lambda ti: (ti, 0)),
            pl.BlockSpec((tm, d), lambda ti: (ti, 0)),
            pl.BlockSpec((1, att.shape[-1]), const),
            pl.BlockSpec((d_mix, d), const),
            pl.BlockSpec((1, d), const),
            pl.BlockSpec(w['w_query'].shape, const),
            pl.BlockSpec((N_KEYS, D_KEY // 2), const),
            pl.BlockSpec((N_KEYS, D_KEY // 2), const),
        ],
        out_specs=[
            pl.BlockSpec((tm, d), lambda ti: (ti, 0)),
            pl.BlockSpec((d, tm), lambda ti: (0, ti)),
            tile4, tile4, tile4p, tile4p,
        ],
        out_shape=[
            jax.ShapeDtypeStruct((n, d), F32),
            jax.ShapeDtypeStruct((d, n), BF16),
            jax.ShapeDtypeStruct(sel_shape, F32),
            jax.ShapeDtypeStruct(sel_shape, F32),
            jax.ShapeDtypeStruct(selp_shape, BF16),
            jax.ShapeDtypeStruct(selp_shape, BF16),
        ],
        scratch_shapes=[
            pltpu.VMEM((2 * PEER_HEADS, nlt, N_KEYS, LANES), F32),
            pltpu.VMEM((2 * PEER_HEADS, nlt, N_KEYS, LANES), F32),
            pltpu.VMEM((2 * PEER_HEADS, nlt, PEER_TOPK, LANES), F32),
        ],
        compiler_params=pltpu.CompilerParams(
            dimension_semantics=("arbitrary",), vmem_limit_bytes=VMEM_LIMIT),
    )(att, cn, x2, w['attn_out_g'], w['w_out'], w['norm2_g'], w['w_query'], w['keys1'], w['keys2'])


def _gelu_tanh(x):
    c = float(np.sqrt(2.0 / np.pi))
    hx = 0.5 * x
    return hx + hx * jnp.tanh(x * (c + (c * 0.044715) * (x * x)))


PACK_ROWS = 16


def _peer_kernel(xnt_ref, u_ref, vt_ref, c_ref, e1_ref, r2_ref, e2_ref, h1_ref, fg_ref, y_ref,
                 acc, pbuf, tbl, *, t, ec):
    e = pl.program_id(1)
    nlt = t // LANES
    nib = ec // N_KEYS
    nsub = ec // SUB_EXPERTS
    kps = SUB_EXPERTS // N_KEYS
    npos = N_KEYS // PACK_ROWS

    @pl.when(e == 0)
    def _():
        acc[...] = jnp.zeros_like(acc)

    row0 = pl.multiple_of(e * nib, nib)
    for h in range(PEER_HEADS):
        for lt in range(nlt):
            ls = slice((lt % 2) * LANES, (lt % 2 + 1) * LANES)
            c8 = c_ref[h, lt, pl.ds(row0, nib), :]
            e8 = e1_ref[h, lt, pl.ds(row0, nib), :]
            for kk in range(nib):
                tbl[0, h, lt // 2, kk, :, ls] = jnp.broadcast_to(c8[kk:kk + 1, :], (PACK_ROWS, LANES)).astype(BF16)
                tbl[1, h, lt // 2, kk, :, ls] = jnp.broadcast_to(e8[kk:kk + 1, :], (PACK_ROWS, LANES)).astype(BF16)

    for sb in range(nsub):
        act = _dot(u_ref[sb * SUB_EXPERTS:(sb + 1) * SUB_EXPERTS, :], xnt_ref[...])
        for k in range(kps):
            for lt2 in range(nlt // 2):
                ls = slice(lt2 * 2 * LANES, (lt2 + 1) * 2 * LANES)
                gate = None
                for h in range(PEER_HEADS):
                    cb = jnp.concatenate([tbl[0, h, lt2, sb * kps + k]] * npos, axis=0)
                    eb = jnp.concatenate([tbl[1, h, lt2, sb * kps + k]] * npos, axis=0)
                    term = jnp.where(r2_ref[h, lt2] < cb, e2_ref[h, lt2] * eb, jnp.zeros((), BF16))
                    gate = term if gate is None else gate + term
                gl = _gelu_tanh(act[k * N_KEYS:(k + 1) * N_KEYS, ls]).astype(BF16)
                pbuf[sb, k * N_KEYS:(k + 1) * N_KEYS, ls] = gate * gl
        acc[...] += _dot(vt_ref[sb], pbuf[sb])

    @pl.when(e == pl.num_programs(1) - 1)
    def _():
        h2 = h1_ref[...] + acc[...].T
        y_ref[...] = _rms(h2, fg_ref[...])


def _peer_call(xnt, u, vt, c, e1, r2, e2, h1, fg, t, ec):
    d, n = xnt.shape
    ne = u.shape[0] // ec
    nsub = ec // SUB_EXPERTS
    assert vt.shape == (ne * nsub, d, SUB_EXPERTS)
    nlt = t // LANES
    tile4 = pl.BlockSpec((PEER_HEADS, nlt, N_KEYS, LANES), lambda ti, ei: (0, ti, 0, 0))
    tile4p = pl.BlockSpec((PEER_HEADS, nlt // 2, N_KEYS, 2 * LANES), lambda ti, ei: (0, ti, 0, 0))
    kern = functools.partial(_peer_kernel, t=t, ec=ec)
    return pl.pallas_call(
        kern,
        name="peer",
        grid=(n // t, ne),
        in_specs=[
            pl.BlockSpec((d, t), lambda ti, ei: (0, ti)),
            pl.BlockSpec((ec, d), lambda ti, ei: (ei, 0)),
            pl.BlockSpec((nsub, d, SUB_EXPERTS), lambda ti, ei: (ei, 0, 0)),
            tile4, tile4, tile4p, tile4p,
            pl.BlockSpec((t, d), lambda ti, ei: (ti, 0)),
            pl.BlockSpec((1, d), lambda ti, ei: (0, 0)),
        ],
        out_specs=pl.BlockSpec((t, d), lambda ti, ei: (ti, 0)),
        out_shape=jax.ShapeDtypeStruct((n, d), F32),
        scratch_shapes=[pltpu.VMEM((d, t), F32), pltpu.VMEM((nsub, SUB_EXPERTS, t), BF16),
                        pltpu.VMEM((2, PEER_HEADS, nlt // 2, ec // N_KEYS, PACK_ROWS, 2 * LANES), BF16)],
        compiler_params=pltpu.CompilerParams(
            dimension_semantics=("arbitrary", "arbitrary"), vmem_limit_bytes=VMEM_LIMIT),
    )(xnt, u, vt, c, e1, r2, e2, h1, fg)


def _rope_slab_tables(length):
    inv = 1.0 / (ROPE_THETA ** (jnp.arange(0, QK_ROPE, 2, dtype=F32) / QK_ROPE))
    ang = jnp.arange(length, dtype=F32)[:, None] * inv[None, :]
    cos, sin = jnp.cos(ang), jnp.sin(ang)
    half = QK_ROPE // 2
    pad = HEAD_SLAB - QK_NOPE - QK_ROPE
    ones = jnp.ones((length, QK_NOPE), F32)
    zeros_n = jnp.zeros((length, QK_NOPE), F32)
    zeros_p = jnp.zeros((length, pad), F32)
    del half
    cos_t = jnp.concatenate([ones, cos, cos, zeros_p], axis=1)
    sin_t = jnp.concatenate([zeros_n, -sin, sin, zeros_p], axis=1)
    return cos_t, sin_t


def _head_slabs(wcols, n_heads, width, offset=0):
    k = wcols.shape[0]
    w3 = wcols.reshape(k, n_heads, width)

    def at(off):
        return jnp.pad(w3, ((0, 0), (0, 0), (off, HEAD_SLAB - width - off)))

    if isinstance(offset, int):
        return at(offset).reshape(k, n_heads * HEAD_SLAB)
    out = None
    for off in sorted(set(offset)):
        pick = jnp.asarray([o == off for o in offset])[None, :, None]
        out = jnp.where(pick, at(off), 0.0) if out is None else jnp.where(pick, at(off), out)
    return out.reshape(k, n_heads * HEAD_SLAB)


def _prep_weights(norm1_g, w_in, q_norm_g, w_uq, kv_norm_g, w_ukv, conv_w, conv_b, attn_out_g,
                  conv_out_g, w_out, norm2_g, peer_w_query, peer_keys1, peer_keys2):
    half = QK_ROPE // 2
    o1 = Q_LORA
    o2 = o1 + KV_LORA
    o3 = o2 + QK_ROPE
    k_in = w_in.shape[0]
    kr = w_in[:, o2:o3]
    zpad_n = jnp.zeros((k_in, QK_NOPE), F32)
    zpad_p = jnp.zeros((k_in, HEAD_SLAB - QK_NOPE - QK_ROPE), F32)
    kr_a = jnp.concatenate([zpad_n, kr[:, :half], kr[:, half:], zpad_p], axis=1)
    kr_b = jnp.concatenate([zpad_n, kr[:, half:], kr[:, :half], zpad_p], axis=1)
    w_in_ext = jnp.concatenate([w_in[:, :o2], kr_a, kr_b, w_in[:, o3:]], axis=1).astype(BF16)

    qd = QK_NOPE + QK_ROPE
    wq3 = w_uq.reshape(Q_LORA, ATT_HEADS, qd)
    wq_swap = jnp.concatenate([wq3[:, :, :QK_NOPE], wq3[:, :, QK_NOPE + half:], wq3[:, :, QK_NOPE:QK_NOPE + half]],
                              axis=2).reshape(Q_LORA, ATT_HEADS * qd)
    w_q = jnp.concatenate([_head_slabs(w_uq, ATT_HEADS, qd), _head_slabs(wq_swap, ATT_HEADS, qd)],
                          axis=1).astype(BF16)

    wkv3 = w_ukv.reshape(KV_LORA, ATT_HEADS, QK_NOPE + V_HEAD)
    wk = wkv3[:, :, :QK_NOPE].reshape(KV_LORA, ATT_HEADS * QK_NOPE)
    wv = wkv3[:, :, QK_NOPE:].reshape(KV_LORA, ATT_HEADS * V_HEAD)
    v_off = [_v_lane(h) for h in range(ATT_HEADS)]
    w_kv = jnp.concatenate([_head_slabs(wk, ATT_HEADS, QK_NOPE), _head_slabs(wv, ATT_HEADS, V_HEAD, v_off)],
                           axis=1).astype(BF16)
    one_cols = np.zeros((1, ATT_HEADS * HEAD_SLAB), np.float32)
    for h in range(ATT_HEADS):
        one_cols[0, h * HEAD_SLAB + _one_lane(h)] = 1.0
    return {
        'v_one': jnp.asarray(one_cols),
        'norm1_g': norm1_g.reshape(1, -1), 'w_in': w_in_ext,
        'q_norm_g': q_norm_g.reshape(1, -1), 'w_q': w_q,
        'kv_norm_g': kv_norm_g.reshape(1, -1), 'w_kv': w_kv,
        'conv_w': conv_w, 'conv_b': conv_b.reshape(1, -1), 'conv_out_g': conv_out_g.reshape(1, -1),
        'attn_out_g': attn_out_g.reshape(1, -1), 'w_out': w_out.astype(BF16),
        'norm2_g': norm2_g.reshape(1, -1), 'w_query': peer_w_query.astype(BF16),
        'keys1': peer_keys1.astype(BF16), 'keys2': peer_keys2.astype(BF16),
    }


def kernel(x, meta_tokens, norm1_g, w_in, q_norm_g, w_uq, kv_norm_g, w_ukv, conv_w, conv_b, attn_out_g,
           conv_out_g, w_out, norm2_g, peer_w_query, peer_keys1, peer_keys2, peer_u, peer_v, final_g):
    b, s, d = x.shape
    assert norm1_g.shape[0] == 1, "single-layer kernel"
    assert meta_tokens.shape[0] == N_META
    tm = min(TM_PROJ, s)
    assert s % tm == 0 and s % LANES == 0
    w = _prep_weights(norm1_g[0], w_in[0], q_norm_g[0], w_uq[0], kv_norm_g[0], w_ukv[0], conv_w[0],
                      conv_b[0], attn_out_g[0], conv_out_g[0], w_out[0], norm2_g[0], peer_w_query[0],
                      peer_keys1[0], peer_keys2[0])
    d_conv = conv_b.shape[-1]
    cos_t, sin_t = _rope_slab_tables(N_META + s)

    _, k_m, v_m, _, z_m = _proj_call(meta_tokens[None], jnp.zeros((8, d_conv), F32), w,
                                     cos_t[:N_META], sin_t[:N_META], N_META)
    q, k, v, cn, _ = _proj_call(x, z_m[0], w, cos_t[N_META:], sin_t[N_META:], tm)
    hw = ATT_HEADS * HEAD_SLAB
    km = jnp.pad(k_m[0], ((0, LANES - N_META), (0, 0)))
    vm = jnp.pad(v_m[0], ((0, LANES - N_META), (0, 0)))
    att = _attn_call(q, k, v, km, vm, min(TQ_ATT, s), HEADS_PER_STEP)

    n = b * s
    tmix = min(TM_MIX, n)
    h1, xnt, c, e1, r2, e2 = _mix_call(att.reshape(n, D_ATT), cn.reshape(n, d_conv), x.reshape(n, d), w, tmix)
    u_bf = peer_u[0].astype(BF16)
    vt_bf = peer_v[0].reshape(-1, SUB_EXPERTS, d).transpose(0, 2, 1).astype(BF16)
    y = _peer_call(xnt, u_bf, vt_bf, c, e1, r2, e2, h1, final_g.reshape(1, -1), min(T_PEER, n), E_CHUNK)
    return y.reshape(b, s, d)
```

```python
import functools

import numpy as np
import jax
import jax.numpy as jnp
from jax import lax
from jax.experimental import pallas as pl
from jax.experimental.pallas import tpu as pltpu

F32 = jnp.float32
BF16 = jnp.bfloat16

N_META = 16
EPS = 1e-6
ROPE_THETA = 10000.0
ATT_HEADS = 8
Q_LORA = 256
KV_LORA = 128
QK_NOPE = 64
QK_ROPE = 32
V_HEAD = 64
D_ATT = ATT_HEADS * V_HEAD
PEER_HEADS = 8
N_KEYS = 128
D_KEY = 256
PEER_TOPK = 16
CONV_WIDTH = 3

LANES = 128
HEAD_SLAB = LANES
NEG_BIG = -1e30
LOG2E = 1.4426950408889634
VMEM_LIMIT = 56 * 1024 * 1024

TM_PROJ = 512
TQ_ATT = 512
TK_ATT = 512
HEADS_PER_STEP = 4
TM_MIX = 512
T_PEER = 512
E_CHUNK = 2048
SUB_EXPERTS = 2048


def _dot(a, b):
    return jnp.dot(a, b, preferred_element_type=F32)


def _dot_nt(a, b):
    return lax.dot_general(a, b, (((1,), (1,)), ((), ())), preferred_element_type=F32)


def _rms(x, g):
    y = x * lax.rsqrt(jnp.mean(x * x, axis=-1, keepdims=True) + EPS)
    return y * g


def _proj_kernel(x_ref, zprev_ref, g1_ref, win_ref, qg_ref, wq_ref, kvg_ref, wkv_ref, cw_ref, cb_ref,
                 cg_ref, cos_ref, sin_ref, vone_ref,
                 q_ref, k_ref, v_ref, cn_ref, zt_ref, zbuf, *, tm, d_conv, q_scale):
    t = pl.program_id(1)

    @pl.when(t == 0)
    def _():
        zbuf[0:8, :] = zprev_ref[...]

    x = x_ref[0]
    hn = _rms(x, g1_ref[...]).astype(BF16)
    p = _dot(hn, win_ref[...])
    cos = cos_ref[...]
    sin = sin_ref[...]

    qn = _rms(p[:, :Q_LORA], qg_ref[...]).astype(BF16)
    qq = _dot(qn, wq_ref[...])
    hw = ATT_HEADS * HEAD_SLAB
    for h in range(ATT_HEADS):
        lo = h * HEAD_SLAB
        qh = qq[:, lo:lo + HEAD_SLAB] * cos + qq[:, hw + lo:hw + lo + HEAD_SLAB] * sin
        q_ref[0, :, lo:lo + HEAD_SLAB] = (qh * q_scale).astype(BF16)

    o1 = Q_LORA
    o2 = o1 + KV_LORA
    kvn = _rms(p[:, o1:o2], kvg_ref[...]).astype(BF16)
    kv = _dot(kvn, wkv_ref[...])
    kr = p[:, o2:o2 + HEAD_SLAB] * cos + p[:, o2 + HEAD_SLAB:o2 + 2 * HEAD_SLAB] * sin
    for h in range(ATT_HEADS):
        lo = h * HEAD_SLAB
        k_ref[0, :, lo:lo + HEAD_SLAB] = (kv[:, lo:lo + HEAD_SLAB] + kr).astype(BF16)
    v_ref[0] = (kv[:, hw:] + vone_ref[...]).astype(BF16)

    o3 = o2 + 2 * HEAD_SLAB
    bg = p[:, o3:o3 + d_conv]
    z = p[:, o3 + d_conv:o3 + 2 * d_conv] * p[:, o3 + 2 * d_conv:o3 + 3 * d_conv]
    zbuf[8:8 + tm, :] = z
    conv = (cb_ref[...] + cw_ref[0:1, :] * zbuf[6:6 + tm, :] + cw_ref[1:2, :] * zbuf[7:7 + tm, :]
            + cw_ref[2:3, :] * z)
    cn_ref[0] = _rms(bg * conv, cg_ref[...]).astype(BF16)
    tail = zbuf[tm:tm + 8, :]
    zt_ref[0] = tail
    zbuf[0:8, :] = tail


def _proj_call(x, zprev, w, cos, sin, tm):
    b, s, d = x.shape
    d_conv = w['conv_b'].shape[-1]
    hw = ATT_HEADS * HEAD_SLAB
    nt = s // tm
    q_scale = float((QK_NOPE + QK_ROPE) ** -0.5 * LOG2E)
    const = lambda bi, ti: (0, 0)
    kern = functools.partial(_proj_kernel, tm=tm, d_conv=d_conv, q_scale=q_scale)
    return pl.pallas_call(
        kern,
        name="proj",
        grid=(b, nt),
        in_specs=[
            pl.BlockSpec((1, tm, d), lambda bi, ti: (bi, ti, 0)),
            pl.BlockSpec((8, d_conv), const),
            pl.BlockSpec((1, d), const),
            pl.BlockSpec(w['w_in'].shape, const),
            pl.BlockSpec((1, Q_LORA), const),
            pl.BlockSpec(w['w_q'].shape, const),
            pl.BlockSpec((1, KV_LORA), const),
            pl.BlockSpec(w['w_kv'].shape, const),
            pl.BlockSpec((CONV_WIDTH, d_conv), const),
            pl.BlockSpec((1, d_conv), const),
            pl.BlockSpec((1, d_conv), const),
            pl.BlockSpec((tm, HEAD_SLAB), lambda bi, ti: (ti, 0)),
            pl.BlockSpec((tm, HEAD_SLAB), lambda bi, ti: (ti, 0)),
            pl.BlockSpec((1, hw), const),
        ],
        out_specs=[
            pl.BlockSpec((1, tm, hw), lambda bi, ti: (bi, ti, 0)),
            pl.BlockSpec((1, tm, hw), lambda bi, ti: (bi, ti, 0)),
            pl.BlockSpec((1, tm, hw), lambda bi, ti: (bi, ti, 0)),
            pl.BlockSpec((1, tm, d_conv), lambda bi, ti: (bi, ti, 0)),
            pl.BlockSpec((1, 8, d_conv), lambda bi, ti: (bi, ti, 0)),
        ],
        out_shape=[
            jax.ShapeDtypeStruct((b, s, hw), BF16),
            jax.ShapeDtypeStruct((b, s, hw), BF16),
            jax.ShapeDtypeStruct((b, s, hw), BF16),
            jax.ShapeDtypeStruct((b, s, d_conv), BF16),
            jax.ShapeDtypeStruct((b, nt * 8, d_conv), F32),
        ],
        scratch_shapes=[pltpu.VMEM((tm + 8, d_conv), F32)],
        compiler_params=pltpu.CompilerParams(
            dimension_semantics=("arbitrary", "arbitrary"), vmem_limit_bytes=VMEM_LIMIT),
    )(x, zprev, w['norm1_g'], w['w_in'], w['q_norm_g'], w['w_q'], w['kv_norm_g'], w['w_kv'],
      w['conv_w'], w['conv_b'], w['conv_out_g'], cos, sin, w['v_one'])


def _v_lane(h):
    return V_HEAD * (h % 2)


def _one_lane(h):
    return V_HEAD * ((h + 1) % 2)


def _attn_kernel(q_ref, k_ref, v_ref, km_ref, vm_ref, o_ref, *, tq, tk, hg):
    i = pl.program_id(2)
    nd = tq // tk
    row = lax.broadcasted_iota(jnp.int32, (tq, tk), 0)
    col = lax.broadcasted_iota(jnp.int32, (tq, tk), 1)
    meta_ok = lax.broadcasted_iota(jnp.int32, (tq, LANES), 1) < N_META

    def step(carry, qh, kc, vc, mask):
        m, acc = carry
        s = _dot_nt(qh, kc)
        if mask is not None:
            s = jnp.where(mask, s, NEG_BIG)
        m_new = jnp.maximum(m, jnp.max(s, axis=-1, keepdims=True))
        pr = jnp.exp2(s - m_new)
        acc = jnp.exp2(m - m_new) * acc + _dot(pr.astype(BF16), vc)
        return m_new, acc

    slabs = [slice(hh * HEAD_SLAB, (hh + 1) * HEAD_SLAB) for hh in range(hg)]
    qs = [q_ref[0, :, sl] for sl in slabs]
    init = (jnp.full((tq, 1), NEG_BIG, F32), jnp.zeros((tq, LANES), F32))
    carries = tuple(step(init, qs[hh], km_ref[:, slabs[hh]], vm_ref[:, slabs[hh]], meta_ok)
                    for hh in range(hg))

    def body(j, cs):
        rows = pl.ds(pl.multiple_of(j * tk, tk), tk)
        return tuple(step(cs[hh], qs[hh], k_ref[0, rows, slabs[hh]], v_ref[0, rows, slabs[hh]], None)
                     for hh in range(hg))

    carries = lax.fori_loop(0, i * nd, body, carries)
    for c in range(nd):
        rows = pl.ds(pl.multiple_of(i * tq + c * tk, tk), tk)
        causal = col + c * tk <= row
        carries = tuple(step(carries[hh], qs[hh], k_ref[0, rows, slabs[hh]], v_ref[0, rows, slabs[hh]],
                             causal) for hh in range(hg))
    lane = lax.broadcasted_iota(jnp.int32, (tq, LANES), 1)
    outs = []
    for hh in range(hg):
        one = _one_lane(hh)
        acc = carries[hh][1]
        outs.append(jnp.where(lane == one, 0.0, acc / acc[:, one:one + 1]))
    o_ref[0] = jnp.concatenate([outs[2 * p] + outs[2 * p + 1] for p in range(hg // 2)], axis=-1).astype(BF16)


def _attn_call(q, k, v, km, vm, tq, hg):
    b, s, hw = q.shape
    gw = hg * HEAD_SLAB
    ow = (hg // 2) * LANES
    tk = min(TK_ATT, tq)
    kern = functools.partial(_attn_kernel, tq=tq, tk=tk, hg=hg)
    return pl.pallas_call(
        kern,
        name="attn",
        grid=(b, ATT_HEADS // hg, s // tq),
        in_specs=[
            pl.BlockSpec((1, tq, gw), lambda bi, gi, qi: (bi, qi, gi)),
            pl.BlockSpec((1, s, gw), lambda bi, gi, qi: (bi, 0, gi)),
            pl.BlockSpec((1, s, gw), lambda bi, gi, qi: (bi, 0, gi)),
            pl.BlockSpec((LANES, gw), lambda bi, gi, qi: (0, gi)),
            pl.BlockSpec((LANES, gw), lambda bi, gi, qi: (0, gi)),
        ],
        out_specs=pl.BlockSpec((1, tq, ow), lambda bi, gi, qi: (bi, qi, gi)),
        out_shape=jax.ShapeDtypeStruct((b, s, D_ATT), BF16),
        compiler_params=pltpu.CompilerParams(
            dimension_semantics=("arbitrary", "arbitrary", "arbitrary"), vmem_limit_bytes=VMEM_LIMIT),
    )(q, k, v, km, vm)


def _rank16_by_extraction(s):
    key = lax.broadcasted_iota(jnp.int32, s.shape, 0)
    rank = jnp.full(s.shape, float(PEER_TOPK), F32)
    for r in range(PEER_TOPK):
        m = jnp.max(s, axis=0, keepdims=True)
        first = jnp.min(jnp.where(s == m, key, N_KEYS), axis=0, keepdims=True)
        hit = key == first
        rank = jnp.where(hit, float(r), rank)
        s = jnp.where(hit, -jnp.inf, s)
    return rank


def _oddeven_merge(lo, hi, r):
    step = r * 2
    if step < hi - lo:
        yield from _oddeven_merge(lo, hi, step)
        yield from _oddeven_merge(lo + r, hi, step)
        yield from [(i, i + r) for i in range(lo + r, hi - r, step)]
    else:
        yield (lo, lo + r)


def _oddeven_merge_sort(lo, hi):
    if hi - lo >= 1:
        mid = lo + (hi - lo) // 2
        yield from _oddeven_merge_sort(lo, mid)
        yield from _oddeven_merge_sort(mid + 1, hi)
        yield from _oddeven_merge(lo, hi, 1)


_SORT16_NET = tuple(_oddeven_merge_sort(0, PEER_TOPK - 1))
SUBLANES = 8


def _sorted_top16(tiles):
    cols = list(tiles) + [None] * (PEER_TOPK - len(tiles))

    def vmax(x, y):
        return y if x is None else x if y is None else jnp.maximum(x, y)

    def vmin(x, y):
        return None if x is None or y is None else jnp.minimum(x, y)

    def exchange(i, j):
        cols[i], cols[j] = vmax(cols[i], cols[j]), vmin(cols[i], cols[j])

    for i, j in _SORT16_NET:
        exchange(i, j)
    for shift in (4, 2, 1):
        partner = [None if c is None else pltpu.roll(c, shift, axis=0) for c in reversed(cols)]
        cols = [vmax(cols[r], partner[r]) for r in range(PEER_TOPK)]
        for stride in (8, 4, 2, 1):
            for r in range(PEER_TOPK):
                if r & stride == 0:
                    exchange(r, r + stride)
    return cols


def _top16_tile(s, want_rank):
    nv = N_KEYS // SUBLANES
    rows = [s[SUBLANES * v:SUBLANES * (v + 1), :] for v in range(nv)]
    top = _sorted_top16(rows)
    vals = jnp.concatenate([t[0:1, :] for t in top], axis=0)
    at_least = jnp.zeros((SUBLANES, LANES), F32)
    for v in range(nv):
        at_least = at_least + jnp.where(rows[v] >= top[PEER_TOPK - 1], 1.0, 0.0)
    tied = jnp.where(jnp.sum(at_least, axis=0, keepdims=True) != float(PEER_TOPK), 1.0, 0.0)
    for r in range(PEER_TOPK - 1):
        tied = tied + jnp.where(top[r][0:1, :] <= top[r + 1][0:1, :], 1.0, 0.0)
    if not want_rank:
        return vals, None, tied
    rank = []
    for v in range(nv):
        rv = jnp.zeros((SUBLANES, LANES), F32)
        for r in range(PEER_TOPK):
            rv = rv + jnp.where(top[r] > rows[v], 1.0, 0.0)
        rank.append(rv)
    return vals, jnp.concatenate(rank, axis=0), tied


def _candidate_blocks():
    blocks = []
    for a in range(8):
        nb = PEER_TOPK // (a + 1)
        for b0 in range(0, nb, 8):
            nv = min(8, nb - b0)
            flat = np.array([a * PEER_TOPK + b0 + r for r in range(8)])
            blocks.append(('row', a, b0, nv, flat))
    flat = np.array([(8 + r) * PEER_TOPK for r in range(8)])
    blocks.append(('col', 8, 0, 8, flat))
    return blocks


_CAND_BLOCKS = _candidate_blocks()


def _select_counts(v1, v2, exact):
    sub = lax.broadcasted_iota(jnp.int32, (8, LANES), 0)
    vals = []
    for kind, a0, b0, nv, _ in _CAND_BLOCKS:
        if kind == 'row':
            blk = v1[a0:a0 + 1, :] + v2[b0:b0 + 8, :]
        else:
            blk = v1[a0:a0 + 8, :] + v2[b0:b0 + 1, :]
        if nv < 8:
            blk = jnp.where(sub < nv, blk, -jnp.inf)
        vals.append(blk)
    def by_pairwise_rank(blocks):
        beaten = [jnp.zeros((8, LANES), F32) for _ in blocks]
        for si, (_, _, _, nv_s, flat_s) in enumerate(_CAND_BLOCKS):
            for r in range(nv_s):
                src = jnp.broadcast_to(blocks[si][r:r + 1, :], (8, LANES))
                for ti, (_, _, _, _, flat_t) in enumerate(_CAND_BLOCKS):
                    wins_tie = flat_s[r] < flat_t
                    gt = jnp.where(src > blocks[ti], 1.0, 0.0)
                    ge = jnp.where(src >= blocks[ti], 1.0, 0.0)
                    if wins_tie.all():
                        beats = ge
                    elif not wins_tie.any():
                        beats = gt
                    else:
                        beats = jnp.where(sub >= int((~wins_tie).sum()), ge, gt)
                    beaten[ti] = beaten[ti] + beats
        return tuple(jnp.where(b < float(PEER_TOPK), 1.0, 0.0) for b in beaten)

    if exact:
        selected = by_pairwise_rank(vals)
        tied = jnp.zeros((1, LANES), F32)
    else:
        kth = _sorted_top16(vals)[PEER_TOPK - 1]
        selected = tuple(jnp.where(blk >= kth, 1.0, 0.0) for blk in vals)
        n_reach = jnp.sum(sum(selected[1:], selected[0]), axis=0, keepdims=True)
        tied = jnp.where(n_reach != float(PEER_TOPK), 1.0, 0.0)

    top = vals[0][0:1, :]
    z = jnp.zeros((1, LANES), F32)
    cnt_rows = [jnp.zeros((1, LANES), F32) for _ in range(PEER_TOPK)]
    for ti, (kind, a0, b0, nv, _) in enumerate(_CAND_BLOCKS):
        self_f = selected[ti]
        if nv < 8:
            self_f = jnp.where(sub < nv, self_f, 0.0)
        z = z + jnp.sum(self_f * jnp.exp(vals[ti] - top), axis=0, keepdims=True)
        if kind == 'row':
            cnt_rows[a0] = cnt_rows[a0] + jnp.sum(self_f, axis=0, keepdims=True)
        else:
            for r in range(8):
                cnt_rows[a0 + r] = cnt_rows[a0 + r] + self_f[r:r + 1, :]
    return jnp.concatenate(cnt_rows, axis=0), z, tied


def _mix_kernel(att_ref, cn_ref, x_ref, ag_ref, wo_ref, g2_ref, wqr_ref, k1_ref, k2_ref,
                h1_ref, xnt_ref, c_ref, e1_ref, r2_ref, e2_ref, st, rk, vs, *, tm):
    nlt = tm // LANES
    att = _rms(att_ref[...].astype(F32), ag_ref[...]).astype(BF16)
    d_att = att.shape[-1]
    h1 = x_ref[...] + _dot(att, wo_ref[0:d_att, :]) + _dot(cn_ref[...], wo_ref[d_att:, :])
    h1_ref[...] = h1
    xn32 = _rms(h1, g2_ref[...])
    xn = xn32.astype(BF16)
    xnt_ref[...] = xn32.T.astype(BF16)
    qp = _dot(xn, wqr_ref[...]).astype(BF16)
    half = D_KEY // 2
    for h in range(PEER_HEADS):
        for side, kref in enumerate((k1_ref, k2_ref)):
            lo = h * D_KEY + side * half
            sc = _dot_nt(kref[...], qp[:, lo:lo + half])
            for lt in range(nlt):
                st[2 * h + side, lt] = sc[:, lt * LANES:(lt + 1) * LANES]

    no_tie = jnp.zeros((1, LANES), F32)

    def rank_body(idx, tied):
        h = idx // nlt
        lt = idx % nlt
        vals1, _, t1 = _top16_tile(st[2 * h, lt], False)
        vals2, rank2, t2 = _top16_tile(st[2 * h + 1, lt], True)
        vs[2 * h, lt] = vals1
        vs[2 * h + 1, lt] = vals2
        rk[2 * h + 1, lt] = rank2
        return jnp.maximum(tied, jnp.maximum(t1, t2))

    def rank_exact_body(idx, carry):
        p = idx // nlt
        lt = idx % nlt
        rk[p, lt] = _rank16_by_extraction(st[p, lt])
        return carry

    def sel_body(exact, idx, tied):
        h = idx // (nlt // 2)
        lt2 = idx % (nlt // 2)
        for half in range(2):
            lt = 2 * lt2 + half
            ls = slice(half * LANES, (half + 1) * LANES)
            v1 = vs[2 * h, lt]
            v2 = vs[2 * h + 1, lt]
            s1 = st[2 * h, lt]
            cnt, z, t = _select_counts(v1, v2, exact)
            tied = jnp.maximum(tied, t)
            r1 = rk[2 * h, lt] if exact else None
            c = jnp.zeros((N_KEYS, LANES), F32)
            for a in range(PEER_TOPK):
                hit = (r1 == float(a)) if exact else (s1 == v1[a:a + 1, :])
                c = c + jnp.where(hit, cnt[a:a + 1, :], 0.0)
            c_ref[h, lt] = c
            e1_ref[h, lt] = jnp.exp(s1 - v1[0:1, :]) * (1.0 / z)
            r2_ref[h, lt2, :, ls] = rk[2 * h + 1, lt].astype(BF16)
            e2_ref[h, lt2, :, ls] = jnp.exp(st[2 * h + 1, lt] - v2[0:1, :]).astype(BF16)
        return tied

    tied = lax.fori_loop(0, PEER_HEADS * nlt, rank_body, no_tie)
    tied = lax.fori_loop(0, PEER_HEADS * (nlt // 2), functools.partial(sel_body, False), tied)

    @pl.when(jnp.max(tied) > 0.0)
    def _():
        lax.fori_loop(0, 2 * PEER_HEADS * nlt, rank_exact_body, 0)
        lax.fori_loop(0, PEER_HEADS * (nlt // 2), functools.partial(sel_body, True), no_tie)


def _mix_call(att, cn, x2, w, tm):
    n, d = x2.shape
    nlt = tm // LANES
    d_mix = w['w_out'].shape[0]
    const = lambda ti: (0, 0)
    tile4 = pl.BlockSpec((PEER_HEADS, nlt, N_KEYS, LANES), lambda ti: (0, ti, 0, 0))
    tile4p = pl.BlockSpec((PEER_HEADS, nlt // 2, N_KEYS, 2 * LANES), lambda ti: (0, ti, 0, 0))
    sel_shape = (PEER_HEADS, n // LANES, N_KEYS, LANES)
    selp_shape = (PEER_HEADS, n // (2 * LANES), N_KEYS, 2 * LANES)
    kern = functools.partial(_mix_kernel, tm=tm)
    return pl.pallas_call(
        kern,
        name="mix",
        grid=(n // tm,),
        in_specs=[
            pl.BlockSpec((tm, att.shape[-1]), lambda ti: (ti, 0)),
            pl.BlockSpec((tm, cn.shape[-1]), lambda ti: (ti, 0)),
            pl.BlockSpec((tm, d), lambda ti: (ti, 0)),
            pl.BlockSpec((1, att.shape[-1]), const),
            pl.BlockSpec((d_mix, d), const),
            pl.BlockSpec((1, d), const),
            pl.BlockSpec(w['w_query'].shape, const),
            pl.BlockSpec((N_KEYS, D_KEY // 2), const),
            pl.BlockSpec((N_KEYS, D_KEY // 2), const),
        ],
        out_specs=[
            pl.BlockSpec((tm, d), lambda ti: (ti, 0)),
            pl.BlockSpec((d, tm), lambda ti: (0, ti)),
            tile4, tile4, tile4p, tile4p,
        ],
        out_shape=[
            jax.ShapeDtypeStruct((n, d), F32),
            jax.ShapeDtypeStruct((d, n), BF16),
            jax.ShapeDtypeStruct(sel_shape, F32),
            jax.ShapeDtypeStruct(sel_shape, F32),
            jax.ShapeDtypeStruct(selp_shape, BF16),
            jax.ShapeDtypeStruct(selp_shape, BF16),
        ],
        scratch_shapes=[
            pltpu.VMEM((2 * PEER_HEADS, nlt, N_KEYS, LANES), F32),
            pltpu.VMEM((2 * PEER_HEADS, nlt, N_KEYS, LANES), F32),
            pltpu.VMEM((2 * PEER_HEADS, nlt, PEER_TOPK, LANES), F32),
        ],
        compiler_params=pltpu.CompilerParams(
            dimension_semantics=("arbitrary",), vmem_limit_bytes=VMEM_LIMIT),
    )(att, cn, x2, w['attn_out_g'], w['w_out'], w['norm2_g'], w['w_query'], w['keys1'], w['keys2'])


def _bf16_split(value):
    hi = float(np.asarray(value, dtype=BF16))
    lo = float(np.asarray(value - hi, dtype=BF16))
    return hi, lo


def _gelu_tanh(x):
    c = float(np.sqrt(2.0 / np.pi))
    c_hi, c_lo = _bf16_split(c)
    k_hi, k_lo = _bf16_split(c * 0.044715)
    x2 = x * x
    poly = (x2 * k_hi + c_hi) + (x2 * k_lo + c_lo)
    hx = 0.5 * x
    return hx + hx * jnp.tanh(x * poly)


PACK_ROWS = 16


def _peer_kernel(xnt_ref, u_ref, vt_ref, c_ref, e1_ref, r2_ref, e2_ref, h1_ref, fg_ref, y_ref,
                 acc, pbuf, tbl, *, t, ec):
    e = pl.program_id(1)
    nlt = t // LANES
    nib = ec // N_KEYS
    nsub = ec // SUB_EXPERTS
    kps = SUB_EXPERTS // N_KEYS
    npos = N_KEYS // PACK_ROWS

    @pl.when(e == 0)
    def _():
        acc[...] = jnp.zeros_like(acc)

    row0 = pl.multiple_of(e * nib, nib)
    for h in range(PEER_HEADS):
        for lt in range(nlt):
            ls = slice((lt % 2) * LANES, (lt % 2 + 1) * LANES)
            c8 = c_ref[h, lt, pl.ds(row0, nib), :]
            e8 = e1_ref[h, lt, pl.ds(row0, nib), :]
            for kk in range(nib):
                tbl[0, h, lt // 2, kk, :, ls] = jnp.broadcast_to(c8[kk:kk + 1, :], (PACK_ROWS, LANES)).astype(BF16)
                tbl[1, h, lt // 2, kk, :, ls] = jnp.broadcast_to(e8[kk:kk + 1, :], (PACK_ROWS, LANES)).astype(BF16)

    for sb in range(nsub):
        act = _dot(u_ref[sb * SUB_EXPERTS:(sb + 1) * SUB_EXPERTS, :], xnt_ref[...])
        for k in range(kps):
            for lt2 in range(nlt // 2):
                ls = slice(lt2 * 2 * LANES, (lt2 + 1) * 2 * LANES)
                gate = None
                for h in range(PEER_HEADS):
                    cb = jnp.concatenate([tbl[0, h, lt2, sb * kps + k]] * npos, axis=0)
                    eb = jnp.concatenate([tbl[1, h, lt2, sb * kps + k]] * npos, axis=0)
                    term = jnp.where(r2_ref[h, lt2] < cb, e2_ref[h, lt2] * eb, jnp.zeros((), BF16))
                    gate = term if gate is None else gate + term
                gl = _gelu_tanh(act[k * N_KEYS:(k + 1) * N_KEYS, ls].astype(BF16))
                pbuf[sb, k * N_KEYS:(k + 1) * N_KEYS, ls] = gate * gl
        acc[...] += _dot(vt_ref[sb], pbuf[sb])

    @pl.when(e == pl.num_programs(1) - 1)
    def _():
        h2 = h1_ref[...] + acc[...].T
        y_ref[...] = _rms(h2, fg_ref[...])


def _peer_call(xnt, u, vt, c, e1, r2, e2, h1, fg, t, ec):
    d, n = xnt.shape
    ne = u.shape[0] // ec
    nsub = ec // SUB_EXPERTS
    assert vt.shape == (ne * nsub, d, SUB_EXPERTS)
    nlt = t // LANES
    tile4 = pl.BlockSpec((PEER_HEADS, nlt, N_KEYS, LANES), lambda ti, ei: (0, ti, 0, 0))
    tile4p = pl.BlockSpec((PEER_HEADS, nlt // 2, N_KEYS, 2 * LANES), lambda ti, ei: (0, ti, 0, 0))
    kern = functools.partial(_peer_kernel, t=t, ec=ec)
    return pl.pallas_call(
        kern,
        name="peer",
        grid=(n // t, ne),
        in_specs=[
            pl.BlockSpec((d, t), lambda ti, ei: (0, ti)),
            pl.BlockSpec((ec, d), lambda ti, ei: (ei, 0)),
            pl.BlockSpec((nsub, d, SUB_EXPERTS), lambda ti, ei: (ei, 0, 0)),
            tile4, tile4, tile4p, tile4p,
            pl.BlockSpec((t, d), lambda ti, ei: (ti, 0)),
            pl.BlockSpec((1, d), lambda ti, ei: (0, 0)),
        ],
        out_specs=pl.BlockSpec((t, d), lambda ti, ei: (ti, 0)),
        out_shape=jax.ShapeDtypeStruct((n, d), F32),
        scratch_shapes=[pltpu.VMEM((d, t), F32), pltpu.VMEM((nsub, SUB_EXPERTS, t), BF16),
                        pltpu.VMEM((2, PEER_HEADS, nlt // 2, ec // N_KEYS, PACK_ROWS, 2 * LANES), BF16)],
        compiler_params=pltpu.CompilerParams(
            dimension_semantics=("arbitrary", "arbitrary"), vmem_limit_bytes=VMEM_LIMIT),
    )(xnt, u, vt, c, e1, r2, e2, h1, fg)


def _rope_slab_tables(length):
    inv = 1.0 / (ROPE_THETA ** (jnp.arange(0, QK_ROPE, 2, dtype=F32) / QK_ROPE))
    ang = jnp.arange(length, dtype=F32)[:, None] * inv[None, :]
    cos, sin = jnp.cos(ang), jnp.sin(ang)
    half = QK_ROPE // 2
    pad = HEAD_SLAB - QK_NOPE - QK_ROPE
    ones = jnp.ones((length, QK_NOPE), F32)
    zeros_n = jnp.zeros((length, QK_NOPE), F32)
    zeros_p = jnp.zeros((length, pad), F32)
    del half
    cos_t = jnp.concatenate([ones, cos, cos, zeros_p], axis=1)
    sin_t = jnp.concatenate([zeros_n, -sin, sin, zeros_p], axis=1)
    return cos_t, sin_t


def _head_slabs(wcols, n_heads, width, offset=0):
    k = wcols.shape[0]
    w3 = wcols.reshape(k, n_heads, width)

    def at(off):
        return jnp.pad(w3, ((0, 0), (0, 0), (off, HEAD_SLAB - width - off)))

    if isinstance(offset, int):
        return at(offset).reshape(k, n_heads * HEAD_SLAB)
    out = None
    for off in sorted(set(offset)):
        pick = jnp.asarray([o == off for o in offset])[None, :, None]
        out = jnp.where(pick, at(off), 0.0) if out is None else jnp.where(pick, at(off), out)
    return out.reshape(k, n_heads * HEAD_SLAB)


def _prep_weights(norm1_g, w_in, q_norm_g, w_uq, kv_norm_g, w_ukv, conv_w, conv_b, attn_out_g,
                  conv_out_g, w_out, norm2_g, peer_w_query, peer_keys1, peer_keys2):
    half = QK_ROPE // 2
    o1 = Q_LORA
    o2 = o1 + KV_LORA
    o3 = o2 + QK_ROPE
    k_in = w_in.shape[0]
    kr = w_in[:, o2:o3]
    zpad_n = jnp.zeros((k_in, QK_NOPE), F32)
    zpad_p = jnp.zeros((k_in, HEAD_SLAB - QK_NOPE - QK_ROPE), F32)
    kr_a = jnp.concatenate([zpad_n, kr[:, :half], kr[:, half:], zpad_p], axis=1)
    kr_b = jnp.concatenate([zpad_n, kr[:, half:], kr[:, :half], zpad_p], axis=1)
    w_in_ext = jnp.concatenate([w_in[:, :o2], kr_a, kr_b, w_in[:, o3:]], axis=1).astype(BF16)

    qd = QK_NOPE + QK_ROPE
    wq3 = w_uq.reshape(Q_LORA, ATT_HEADS, qd)
    wq_swap = jnp.concatenate([wq3[:, :, :QK_NOPE], wq3[:, :, QK_NOPE + half:], wq3[:, :, QK_NOPE:QK_NOPE + half]],
                              axis=2).reshape(Q_LORA, ATT_HEADS * qd)
    w_q = jnp.concatenate([_head_slabs(w_uq, ATT_HEADS, qd), _head_slabs(wq_swap, ATT_HEADS, qd)],
                          axis=1).astype(BF16)

    wkv3 = w_ukv.reshape(KV_LORA, ATT_HEADS, QK_NOPE + V_HEAD)
    wk = wkv3[:, :, :QK_NOPE].reshape(KV_LORA, ATT_HEADS * QK_NOPE)
    wv = wkv3[:, :, QK_NOPE:].reshape(KV_LORA, ATT_HEADS * V_HEAD)
    v_off = [_v_lane(h) for h in range(ATT_HEADS)]
    w_kv = jnp.concatenate([_head_slabs(wk, ATT_HEADS, QK_NOPE), _head_slabs(wv, ATT_HEADS, V_HEAD, v_off)],
                           axis=1).astype(BF16)
    one_cols = np.zeros((1, ATT_HEADS * HEAD_SLAB), np.float32)
    for h in range(ATT_HEADS):
        one_cols[0, h * HEAD_SLAB + _one_lane(h)] = 1.0
    return {
        'v_one': jnp.asarray(one_cols),
        'norm1_g': norm1_g.reshape(1, -1), 'w_in': w_in_ext,
        'q_norm_g': q_norm_g.reshape(1, -1), 'w_q': w_q,
        'kv_norm_g': kv_norm_g.reshape(1, -1), 'w_kv': w_kv,
        'conv_w': conv_w, 'conv_b': conv_b.reshape(1, -1), 'conv_out_g': conv_out_g.reshape(1, -1),
        'attn_out_g': attn_out_g.reshape(1, -1), 'w_out': w_out.astype(BF16),
        'norm2_g': norm2_g.reshape(1, -1), 'w_query': peer_w_query.astype(BF16),
        'keys1': peer_keys1.astype(BF16), 'keys2': peer_keys2.astype(BF16),
    }


def kernel(x, meta_tokens, norm1_g, w_in, q_norm_g, w_uq, kv_norm_g, w_ukv, conv_w, conv_b, attn_out_g,
           conv_out_g, w_out, norm2_g, peer_w_query, peer_keys1, peer_keys2, peer_u, peer_v, final_g):
    b, s, d = x.shape
    assert norm1_g.shape[0] == 1, "single-layer kernel"
    assert meta_tokens.shape[0] == N_META
    tm = min(TM_PROJ, s)
    assert s % tm == 0 and s % LANES == 0
    w = _prep_weights(norm1_g[0], w_in[0], q_norm_g[0], w_uq[0], kv_norm_g[0], w_ukv[0], conv_w[0],
                      conv_b[0], attn_out_g[0], conv_out_g[0], w_out[0], norm2_g[0], peer_w_query[0],
                      peer_keys1[0], peer_keys2[0])
    d_conv = conv_b.shape[-1]
    cos_t, sin_t = _rope_slab_tables(N_META + s)

    _, k_m, v_m, _, z_m = _proj_call(meta_tokens[None], jnp.zeros((8, d_conv), F32), w,
                                     cos_t[:N_META], sin_t[:N_META], N_META)
    q, k, v, cn, _ = _proj_call(x, z_m[0], w, cos_t[N_META:], sin_t[N_META:], tm)
    hw = ATT_HEADS * HEAD_SLAB
    km = jnp.pad(k_m[0], ((0, LANES - N_META), (0, 0)))
    vm = jnp.pad(v_m[0], ((0, LANES - N_META), (0, 0)))
    att = _attn_call(q, k, v, km, vm, min(TQ_ATT, s), HEADS_PER_STEP)

    n = b * s
    tmix = min(TM_MIX, n)
    h1, xnt, c, e1, r2, e2 = _mix_call(att.reshape(n, D_ATT), cn.reshape(n, d_conv), x.reshape(n, d), w, tmix)
    u_bf = peer_u[0].astype(BF16)
    vt_bf = peer_v[0].reshape(-1, SUB_EXPERTS, d).transpose(0, 2, 1).astype(BF16)
    y = _peer_call(xnt, u_bf, vt_bf, c, e1, r2, e2, h1, final_g.reshape(1, -1), min(T_PEER, n), E_CHUNK)
    return y.reshape(b, s, d)
```

```python
import functools

import numpy as np
import jax
import jax.numpy as jnp
from jax import lax
from jax.experimental import pallas as pl
from jax.experimental.pallas import tpu as pltpu

F32 = jnp.float32
BF16 = jnp.bfloat16

N_META = 16
EPS = 1e-6
ROPE_THETA = 10000.0
ATT_HEADS = 8
Q_LORA = 256
KV_LORA = 128
QK_NOPE = 64
QK_ROPE = 32
V_HEAD = 64
D_ATT = ATT_HEADS * V_HEAD
PEER_HEADS = 8
N_KEYS = 128
D_KEY = 256
PEER_TOPK = 16
CONV_WIDTH = 3

LANES = 128
SUBLANES = 8
PACK_ROWS = 16
HEAD_SLAB = LANES
NEG_BIG = -1e30
LOG2E = 1.4426950408889634
VMEM_LIMIT = 56 * 1024 * 1024

TM_PROJ = 512
TQ_ATT = 512
TK_ATT = 512
HEADS_PER_STEP = 8
TM_MIX = 512
T_PEER = 512
E_CHUNK = 2048
SUB_EXPERTS = 2048


def _dot(a, b):
    return jnp.dot(a, b, preferred_element_type=F32)


def _dot_nt(a, b):
    return lax.dot_general(a, b, (((1,), (1,)), ((), ())), preferred_element_type=F32)


def _rms(x, g):
    y = x * lax.rsqrt(jnp.mean(x * x, axis=-1, keepdims=True) + EPS)
    return y * g


def _proj_kernel(x_ref, zprev_ref, g1_ref, win_ref, qg_ref, wq_ref, kvg_ref, wkv_ref, cw_ref, cb_ref,
                 cg_ref, cos_ref, sin_ref, vone_ref,
                 q_ref, k_ref, v_ref, cn_ref, zt_ref, zbuf, *, tm, d_conv, q_scale):
    t = pl.program_id(1)

    @pl.when(t == 0)
    def _():
        zbuf[0:8, :] = zprev_ref[...]

    x = x_ref[0]
    hn = _rms(x, g1_ref[...]).astype(BF16)
    p = _dot(hn, win_ref[...])
    cos = cos_ref[...]
    sin = sin_ref[...]

    qn = _rms(p[:, :Q_LORA], qg_ref[...]).astype(BF16)
    qq = _dot(qn, wq_ref[...])
    hw = ATT_HEADS * HEAD_SLAB
    for h in range(ATT_HEADS):
        lo = h * HEAD_SLAB
        qh = qq[:, lo:lo + HEAD_SLAB] * cos + qq[:, hw + lo:hw + lo + HEAD_SLAB] * sin
        q_ref[0, :, lo:lo + HEAD_SLAB] = (qh * q_scale).astype(BF16)

    o1 = Q_LORA
    o2 = o1 + KV_LORA
    kvn = _rms(p[:, o1:o2], kvg_ref[...]).astype(BF16)
    kv = _dot(kvn, wkv_ref[...])
    kr = p[:, o2:o2 + HEAD_SLAB] * cos + p[:, o2 + HEAD_SLAB:o2 + 2 * HEAD_SLAB] * sin
    for h in range(ATT_HEADS):
        lo = h * HEAD_SLAB
        k_ref[0, :, lo:lo + HEAD_SLAB] = (kv[:, lo:lo + HEAD_SLAB] + kr).astype(BF16)
    v_ref[0] = (kv[:, hw:] + vone_ref[...]).astype(BF16)

    o3 = o2 + 2 * HEAD_SLAB
    bg = p[:, o3:o3 + d_conv]
    z = p[:, o3 + d_conv:o3 + 2 * d_conv] * p[:, o3 + 2 * d_conv:o3 + 3 * d_conv]
    zbuf[8:8 + tm, :] = z
    conv = (cb_ref[...] + cw_ref[0:1, :] * zbuf[6:6 + tm, :] + cw_ref[1:2, :] * zbuf[7:7 + tm, :]
            + cw_ref[2:3, :] * z)
    cn_ref[0] = _rms(bg * conv, cg_ref[...]).astype(BF16)
    tail = zbuf[tm:tm + 8, :]
    zt_ref[0] = tail
    zbuf[0:8, :] = tail


def _proj_call(x, zprev, w, cos, sin, tm):
    b, s, d = x.shape
    d_conv = w['conv_b'].shape[-1]
    hw = ATT_HEADS * HEAD_SLAB
    nt = s // tm
    q_scale = float((QK_NOPE + QK_ROPE) ** -0.5 * LOG2E)
    const = lambda bi, ti: (0, 0)
    kern = functools.partial(_proj_kernel, tm=tm, d_conv=d_conv, q_scale=q_scale)
    return pl.pallas_call(
        kern,
        name="proj",
        grid=(b, nt),
        in_specs=[
            pl.BlockSpec((1, tm, d), lambda bi, ti: (bi, ti, 0)),
            pl.BlockSpec((8, d_conv), const),
            pl.BlockSpec((1, d), const),
            pl.BlockSpec(w['w_in'].shape, const),
            pl.BlockSpec((1, Q_LORA), const),
            pl.BlockSpec(w['w_q'].shape, const),
            pl.BlockSpec((1, KV_LORA), const),
            pl.BlockSpec(w['w_kv'].shape, const),
            pl.BlockSpec((CONV_WIDTH, d_conv), const),
            pl.BlockSpec((1, d_conv), const),
            pl.BlockSpec((1, d_conv), const),
            pl.BlockSpec((tm, HEAD_SLAB), lambda bi, ti: (ti, 0)),
            pl.BlockSpec((tm, HEAD_SLAB), lambda bi, ti: (ti, 0)),
            pl.BlockSpec((1, hw), const),
        ],
        out_specs=[
            pl.BlockSpec((1, tm, hw), lambda bi, ti: (bi, ti, 0)),
            pl.BlockSpec((1, tm, hw), lambda bi, ti: (bi, ti, 0)),
            pl.BlockSpec((1, tm, hw), lambda bi, ti: (bi, ti, 0)),
            pl.BlockSpec((1, tm, d_conv), lambda bi, ti: (bi, ti, 0)),
            pl.BlockSpec((1, 8, d_conv), lambda bi, ti: (bi, ti, 0)),
        ],
        out_shape=[
            jax.ShapeDtypeStruct((b, s, hw), BF16),
            jax.ShapeDtypeStruct((b, s, hw), BF16),
            jax.ShapeDtypeStruct((b, s, hw), BF16),
            jax.ShapeDtypeStruct((b, s, d_conv), BF16),
            jax.ShapeDtypeStruct((b, nt * 8, d_conv), F32),
        ],
        scratch_shapes=[pltpu.VMEM((tm + 8, d_conv), F32)],
        compiler_params=pltpu.CompilerParams(
            dimension_semantics=("arbitrary", "arbitrary"), vmem_limit_bytes=VMEM_LIMIT),
    )(x, zprev, w['norm1_g'], w['w_in'], w['q_norm_g'], w['w_q'], w['kv_norm_g'], w['w_kv'],
      w['conv_w'], w['conv_b'], w['conv_out_g'], cos, sin, w['v_one'])


def _v_lane(h):
    return V_HEAD * (h % 2)


def _one_lane(h):
    return V_HEAD * ((h + 1) % 2)


def _attn_kernel(q_ref, k_ref, v_ref, km_ref, vm_ref, o_ref, *, tq, tk, hg):
    i = pl.program_id(2)
    nd = tq // tk
    row = lax.broadcasted_iota(jnp.int32, (tq, tk), 0)
    col = lax.broadcasted_iota(jnp.int32, (tq, tk), 1)
    meta_ok = lax.broadcasted_iota(jnp.int32, (tq, LANES), 1) < N_META

    def step(carry, qh, kc, vc, mask):
        m, acc = carry
        s = _dot_nt(qh, kc)
        if mask is not None:
            s = jnp.where(mask, s, NEG_BIG)
        m_new = jnp.maximum(m, jnp.max(s, axis=-1, keepdims=True))
        pr = jnp.exp2(s - m_new)
        acc = jnp.exp2(m - m_new) * acc + _dot(pr.astype(BF16), vc)
        return m_new, acc

    slabs = [slice(hh * HEAD_SLAB, (hh + 1) * HEAD_SLAB) for hh in range(hg)]
    qs = [q_ref[0, :, sl] for sl in slabs]
    init = (jnp.full((tq, 1), NEG_BIG, F32), jnp.zeros((tq, LANES), F32))
    carries = tuple(step(init, qs[hh], km_ref[:, slabs[hh]], vm_ref[:, slabs[hh]], meta_ok)
                    for hh in range(hg))

    def body(j, cs):
        rows = pl.ds(pl.multiple_of(j * tk, tk), tk)
        return tuple(step(cs[hh], qs[hh], k_ref[0, rows, slabs[hh]], v_ref[0, rows, slabs[hh]], None)
                     for hh in range(hg))

    carries = lax.fori_loop(0, i * nd, body, carries)
    for c in range(nd):
        rows = pl.ds(pl.multiple_of(i * tq + c * tk, tk), tk)
        causal = col + c * tk <= row
        carries = tuple(step(carries[hh], qs[hh], k_ref[0, rows, slabs[hh]], v_ref[0, rows, slabs[hh]],
                             causal) for hh in range(hg))
    lane = lax.broadcasted_iota(jnp.int32, (tq, LANES), 1)
    outs = []
    for hh in range(hg):
        one = _one_lane(hh)
        acc = carries[hh][1]
        outs.append(jnp.where(lane == one, 0.0, acc / acc[:, one:one + 1]))
    o_ref[0] = jnp.concatenate([outs[2 * p] + outs[2 * p + 1] for p in range(hg // 2)], axis=-1).astype(BF16)


def _attn_call(q, k, v, km, vm, tq, hg):
    b, s, hw = q.shape
    gw = hg * HEAD_SLAB
    ow = (hg // 2) * LANES
    tk = min(TK_ATT, tq)
    kern = functools.partial(_attn_kernel, tq=tq, tk=tk, hg=hg)
    return pl.pallas_call(
        kern,
        name="attn",
        grid=(b, ATT_HEADS // hg, s // tq),
        in_specs=[
            pl.BlockSpec((1, tq, gw), lambda bi, gi, qi: (bi, qi, gi)),
            pl.BlockSpec((1, s, gw), lambda bi, gi, qi: (bi, 0, gi)),
            pl.BlockSpec((1, s, gw), lambda bi, gi, qi: (bi, 0, gi)),
            pl.BlockSpec((LANES, gw), lambda bi, gi, qi: (0, gi)),
            pl.BlockSpec((LANES, gw), lambda bi, gi, qi: (0, gi)),
        ],
        out_specs=pl.BlockSpec((1, tq, ow), lambda bi, gi, qi: (bi, qi, gi)),
        out_shape=jax.ShapeDtypeStruct((b, s, D_ATT), BF16),
        compiler_params=pltpu.CompilerParams(
            dimension_semantics=("arbitrary", "arbitrary", "arbitrary"), vmem_limit_bytes=VMEM_LIMIT),
    )(q, k, v, km, vm)


def _rank16_by_extraction(s):
    key = lax.broadcasted_iota(jnp.int32, s.shape, 0)
    rank = jnp.full(s.shape, float(PEER_TOPK), F32)
    for r in range(PEER_TOPK):
        m = jnp.max(s, axis=0, keepdims=True)
        first = jnp.min(jnp.where(s == m, key, N_KEYS), axis=0, keepdims=True)
        hit = key == first
        rank = jnp.where(hit, float(r), rank)
        s = jnp.where(hit, -jnp.inf, s)
    return rank


def _oddeven_merge(lo, hi, r):
    step = r * 2
    if step < hi - lo:
        yield from _oddeven_merge(lo, hi, step)
        yield from _oddeven_merge(lo + r, hi, step)
        yield from [(i, i + r) for i in range(lo + r, hi - r, step)]
    else:
        yield (lo, lo + r)


def _oddeven_merge_sort(lo, hi):
    if hi - lo >= 1:
        mid = lo + (hi - lo) // 2
        yield from _oddeven_merge_sort(lo, mid)
        yield from _oddeven_merge_sort(mid + 1, hi)
        yield from _oddeven_merge(lo, hi, 1)


_SORT16_NET = tuple(_oddeven_merge_sort(0, PEER_TOPK - 1))


def _sorted_top16(tiles):
    cols = list(tiles) + [None] * (PEER_TOPK - len(tiles))

    def vmax(x, y):
        return y if x is None else x if y is None else jnp.maximum(x, y)

    def vmin(x, y):
        return None if x is None or y is None else jnp.minimum(x, y)

    def exchange(i, j):
        cols[i], cols[j] = vmax(cols[i], cols[j]), vmin(cols[i], cols[j])

    for i, j in _SORT16_NET:
        exchange(i, j)
    for shift in (4, 2, 1):
        partner = [None if c is None else pltpu.roll(c, shift, axis=0) for c in reversed(cols)]
        cols = [vmax(cols[r], partner[r]) for r in range(PEER_TOPK)]
        for stride in (8, 4, 2, 1):
            for r in range(PEER_TOPK):
                if r & stride == 0:
                    exchange(r, r + stride)
    return cols


def _top16_tile(s, want_rank):
    nv = N_KEYS // SUBLANES
    rows = [s[SUBLANES * v:SUBLANES * (v + 1), :] for v in range(nv)]
    top = _sorted_top16(rows)
    vals = jnp.concatenate([t[0:1, :] for t in top], axis=0)
    at_least = jnp.zeros((SUBLANES, LANES), F32)
    for v in range(nv):
        at_least = at_least + jnp.where(rows[v] >= top[PEER_TOPK - 1], 1.0, 0.0)
    tied = jnp.where(jnp.sum(at_least, axis=0, keepdims=True) != float(PEER_TOPK), 1.0, 0.0)
    for r in range(PEER_TOPK - 1):
        tied = tied + jnp.where(top[r][0:1, :] <= top[r + 1][0:1, :], 1.0, 0.0)
    if not want_rank:
        return vals, None, tied
    rank = []
    for v in range(nv):
        rv = jnp.zeros((SUBLANES, LANES), F32)
        for r in range(PEER_TOPK):
            rv = rv + jnp.where(top[r] > rows[v], 1.0, 0.0)
        rank.append(rv)
    return vals, jnp.concatenate(rank, axis=0), tied


def _candidate_blocks():
    blocks = []
    for a in range(8):
        nb = PEER_TOPK // (a + 1)
        for b0 in range(0, nb, 8):
            nv = min(8, nb - b0)
            flat = np.array([a * PEER_TOPK + b0 + r for r in range(8)])
            blocks.append(('row', a, b0, nv, flat))
    flat = np.array([(8 + r) * PEER_TOPK for r in range(8)])
    blocks.append(('col', 8, 0, 8, flat))
    return blocks


_CAND_BLOCKS = _candidate_blocks()


def _select_counts(v1, v2, exact):
    sub = lax.broadcasted_iota(jnp.int32, (8, LANES), 0)
    vals = []
    for kind, a0, b0, nv, _ in _CAND_BLOCKS:
        if kind == 'row':
            blk = v1[a0:a0 + 1, :] + v2[b0:b0 + 8, :]
        else:
            blk = v1[a0:a0 + 8, :] + v2[b0:b0 + 1, :]
        if nv < 8:
            blk = jnp.where(sub < nv, blk, -jnp.inf)
        vals.append(blk)
    def by_pairwise_rank(blocks):
        beaten = [jnp.zeros((8, LANES), F32) for _ in blocks]
        for si, (_, _, _, nv_s, flat_s) in enumerate(_CAND_BLOCKS):
            for r in range(nv_s):
                src = jnp.broadcast_to(blocks[si][r:r + 1, :], (8, LANES))
                for ti, (_, _, _, _, flat_t) in enumerate(_CAND_BLOCKS):
                    wins_tie = flat_s[r] < flat_t
                    gt = jnp.where(src > blocks[ti], 1.0, 0.0)
                    ge = jnp.where(src >= blocks[ti], 1.0, 0.0)
                    if wins_tie.all():
                        beats = ge
                    elif not wins_tie.any():
                        beats = gt
                    else:
                        beats = jnp.where(sub >= int((~wins_tie).sum()), ge, gt)
                    beaten[ti] = beaten[ti] + beats
        return tuple(jnp.where(b < float(PEER_TOPK), 1.0, 0.0) for b in beaten)

    if exact:
        selected = by_pairwise_rank(vals)
        tied = jnp.zeros((1, LANES), F32)
    else:
        kth = _sorted_top16(vals)[PEER_TOPK - 1]
        selected = tuple(jnp.where(blk >= kth, 1.0, 0.0) for blk in vals)
        n_reach = jnp.sum(sum(selected[1:], selected[0]), axis=0, keepdims=True)
        tied = jnp.where(n_reach != float(PEER_TOPK), 1.0, 0.0)

    top = vals[0][0:1, :]
    z = jnp.zeros((1, LANES), F32)
    cnt_rows = [jnp.zeros((1, LANES), F32) for _ in range(PEER_TOPK)]
    for ti, (kind, a0, b0, nv, _) in enumerate(_CAND_BLOCKS):
        self_f = selected[ti]
        if nv < 8:
            self_f = jnp.where(sub < nv, self_f, 0.0)
        z = z + jnp.sum(self_f * jnp.exp(vals[ti] - top), axis=0, keepdims=True)
        if kind == 'row':
            cnt_rows[a0] = cnt_rows[a0] + jnp.sum(self_f, axis=0, keepdims=True)
        else:
            for r in range(8):
                cnt_rows[a0 + r] = cnt_rows[a0 + r] + self_f[r:r + 1, :]
    return jnp.concatenate(cnt_rows, axis=0), z, tied


def _mix_kernel(att_ref, cn_ref, x_ref, ag_ref, wo_ref, g2_ref, wqr_ref, k1_ref, k2_ref,
                h1_ref, xnt_ref, c_ref, e1_ref, r2_ref, e2_ref, st, rk, vs, *, tm):
    nlt = tm // LANES
    att = _rms(att_ref[...].astype(F32), ag_ref[...]).astype(BF16)
    d_att = att.shape[-1]
    h1 = x_ref[...] + _dot(att, wo_ref[0:d_att, :]) + _dot(cn_ref[...], wo_ref[d_att:, :])
    h1_ref[...] = h1
    xn32 = _rms(h1, g2_ref[...])
    xn = xn32.astype(BF16)
    xnt_ref[...] = xn32.T.astype(BF16)
    qp = _dot(xn, wqr_ref[...]).astype(BF16)
    half = D_KEY // 2
    for h in range(PEER_HEADS):
        for side, kref in enumerate((k1_ref, k2_ref)):
            lo = h * D_KEY + side * half
            sc = _dot_nt(kref[...], qp[:, lo:lo + half])
            for lt in range(nlt):
                st[2 * h + side, lt] = sc[:, lt * LANES:(lt + 1) * LANES]

    no_tie = jnp.zeros((1, LANES), F32)

    def rank_body(idx, tied):
        h = idx // nlt
        lt = idx % nlt
        vals1, _, t1 = _top16_tile(st[2 * h, lt], False)
        vals2, rank2, t2 = _top16_tile(st[2 * h + 1, lt], True)
        vs[2 * h, lt] = vals1
        vs[2 * h + 1, lt] = vals2
        rk[2 * h + 1, lt] = rank2
        return jnp.maximum(tied, jnp.maximum(t1, t2))

    def rank_exact_body(idx, carry):
        p = idx // nlt
        lt = idx % nlt
        rk[p, lt] = _rank16_by_extraction(st[p, lt])
        return carry

    def sel_body(exact, idx, tied):
        h = idx // (nlt // 2)
        lt2 = idx % (nlt // 2)
        for half in range(2):
            lt = 2 * lt2 + half
            ls = slice(half * LANES, (half + 1) * LANES)
            v1 = vs[2 * h, lt]
            v2 = vs[2 * h + 1, lt]
            s1 = st[2 * h, lt]
            cnt, z, t = _select_counts(v1, v2, exact)
            tied = jnp.maximum(tied, t)
            r1 = rk[2 * h, lt] if exact else None
            c = jnp.zeros((N_KEYS, LANES), F32)
            for a in range(PEER_TOPK):
                hit = (r1 == float(a)) if exact else (s1 == v1[a:a + 1, :])
                c = c + jnp.where(hit, cnt[a:a + 1, :], 0.0)
            c_ref[h, lt] = c
            e1_ref[h, lt] = jnp.exp(s1 - v1[0:1, :]) * (1.0 / z)
            r2_ref[h, lt2, :, ls] = rk[2 * h + 1, lt].astype(BF16)
            e2_ref[h, lt2, :, ls] = jnp.exp(st[2 * h + 1, lt] - v2[0:1, :]).astype(BF16)
        return tied

    tied = lax.fori_loop(0, PEER_HEADS * nlt, rank_body, no_tie)
    tied = lax.fori_loop(0, PEER_HEADS * (nlt // 2), functools.partial(sel_body, False), tied)

    @pl.when(jnp.max(tied) > 0.0)
    def _():
        lax.fori_loop(0, 2 * PEER_HEADS * nlt, rank_exact_body, 0)
        lax.fori_loop(0, PEER_HEADS * (nlt // 2), functools.partial(sel_body, True), no_tie)


def _mix_call(att, cn, x2, w, tm):
    n, d = x2.shape
    nlt = tm // LANES
    d_mix = w['w_out'].shape[0]
    const = lambda ti: (0, 0)
    tile4 = pl.BlockSpec((PEER_HEADS, nlt, N_KEYS, LANES), lambda ti: (0, ti, 0, 0))
    tile4p = pl.BlockSpec((PEER_HEADS, nlt // 2, N_KEYS, 2 * LANES), lambda ti: (0, ti, 0, 0))
    sel_shape = (PEER_HEADS, n // LANES, N_KEYS, LANES)
    selp_shape = (PEER_HEADS, n // (2 * LANES), N_KEYS, 2 * LANES)
    kern = functools.partial(_mix_kernel, tm=tm)
    return pl.pallas_call(
        kern,
        name="mix",
        grid=(n // tm,),
        in_specs=[
            pl.BlockSpec((tm, att.shape[-1]), lambda ti: (ti, 0)),
            pl.BlockSpec((tm, cn.shape[-1]), lambda ti: (ti, 0)),
            pl.BlockSpec((tm, d), lambda ti: (ti, 0)),
            pl.BlockSpec((1, att.shape[-1]), const),
            pl.BlockSpec((d_mix, d), const),
            pl.BlockSpec((1, d), const),
            pl.BlockSpec(w['w_query'].shape, const),
            pl.BlockSpec((N_KEYS, D_KEY // 2), const),
            pl.BlockSpec((N_KEYS, D_KEY // 2), const),
        ],
        out_specs=[
            pl.BlockSpec((tm, d), lambda ti: (ti, 0)),
            pl.BlockSpec((d, tm), lambda ti: (0, ti)),
            tile4, tile4, tile4p, tile4p,
        ],
        out_shape=[
            jax.ShapeDtypeStruct((n, d), F32),
            jax.ShapeDtypeStruct((d, n), BF16),
            jax.ShapeDtypeStruct(sel_shape, F32),
            jax.ShapeDtypeStruct(sel_shape, F32),
            jax.ShapeDtypeStruct(selp_shape, BF16),
            jax.ShapeDtypeStruct(selp_shape, BF16),
        ],
        scratch_shapes=[
            pltpu.VMEM((2 * PEER_HEADS, nlt, N_KEYS, LANES), F32),
            pltpu.VMEM((2 * PEER_HEADS, nlt, N_KEYS, LANES), F32),
            pltpu.VMEM((2 * PEER_HEADS, nlt, PEER_TOPK, LANES), F32),
        ],
        compiler_params=pltpu.CompilerParams(
            dimension_semantics=("arbitrary",), vmem_limit_bytes=VMEM_LIMIT),
    )(att, cn, x2, w['attn_out_g'], w['w_out'], w['norm2_g'], w['w_query'], w['keys1'], w['keys2'])


def _bf16_split(value):
    hi = float(np.asarray(value, dtype=BF16))
    lo = float(np.asarray(value - hi, dtype=BF16))
    return hi, lo


def _gelu_tanh(x):
    c = float(np.sqrt(2.0 / np.pi))
    c_hi, c_lo = _bf16_split(c)
    k_hi, k_lo = _bf16_split(c * 0.044715)
    x2 = x * x
    poly = (x2 * k_hi + c_hi) + (x2 * k_lo + c_lo)
    hx = 0.5 * x
    return hx + hx * jnp.tanh(x * poly)


def _peer_kernel(xnt_ref, u_ref, vt_ref, c_ref, e1_ref, r2_ref, e2_ref, h1_ref, fg_ref, y_ref,
                 acc, pbuf, tbl, *, t, ec):
    e = pl.program_id(1)
    nlt = t // LANES
    nib = ec // N_KEYS
    nsub = ec // SUB_EXPERTS
    kps = SUB_EXPERTS // N_KEYS
    npos = N_KEYS // PACK_ROWS

    @pl.when(e == 0)
    def _():
        acc[...] = jnp.zeros_like(acc)

    row0 = pl.multiple_of(e * nib, nib)
    for h in range(PEER_HEADS):
        for lt in range(nlt):
            ls = slice((lt % 2) * LANES, (lt % 2 + 1) * LANES)
            c8 = c_ref[h, lt, pl.ds(row0, nib), :]
            e8 = e1_ref[h, lt, pl.ds(row0, nib), :]
            for kk in range(nib):
                tbl[0, h, lt // 2, kk, :, ls] = jnp.broadcast_to(c8[kk:kk + 1, :], (PACK_ROWS, LANES)).astype(BF16)
                tbl[1, h, lt // 2, kk, :, ls] = jnp.broadcast_to(e8[kk:kk + 1, :], (PACK_ROWS, LANES)).astype(BF16)

    for sb in range(nsub):
        act = _dot(u_ref[sb * SUB_EXPERTS:(sb + 1) * SUB_EXPERTS, :], xnt_ref[...])
        for k in range(kps):
            for lt2 in range(nlt // 2):
                ls = slice(lt2 * 2 * LANES, (lt2 + 1) * 2 * LANES)
                gate = None
                for h in range(PEER_HEADS):
                    cb = jnp.concatenate([tbl[0, h, lt2, sb * kps + k]] * npos, axis=0)
                    eb = jnp.concatenate([tbl[1, h, lt2, sb * kps + k]] * npos, axis=0)
                    term = jnp.where(r2_ref[h, lt2] < cb, e2_ref[h, lt2] * eb, jnp.zeros((), BF16))
                    gate = term if gate is None else gate + term
                gl = _gelu_tanh(act[k * N_KEYS:(k + 1) * N_KEYS, ls].astype(BF16))
                pbuf[sb, k * N_KEYS:(k + 1) * N_KEYS, ls] = gate * gl
        acc[...] += _dot(vt_ref[sb], pbuf[sb])

    @pl.when(e == pl.num_programs(1) - 1)
    def _():
        h2 = h1_ref[...] + acc[...].T
        y_ref[...] = _rms(h2, fg_ref[...])


def _peer_call(xnt, u, vt, c, e1, r2, e2, h1, fg, t, ec):
    d, n = xnt.shape
    ne = u.shape[0] // ec
    nsub = ec // SUB_EXPERTS
    assert vt.shape == (ne * nsub, d, SUB_EXPERTS)
    nlt = t // LANES
    tile4 = pl.BlockSpec((PEER_HEADS, nlt, N_KEYS, LANES), lambda ti, ei: (0, ti, 0, 0))
    tile4p = pl.BlockSpec((PEER_HEADS, nlt // 2, N_KEYS, 2 * LANES), lambda ti, ei: (0, ti, 0, 0))
    kern = functools.partial(_peer_kernel, t=t, ec=ec)
    return pl.pallas_call(
        kern,
        name="peer",
        grid=(n // t, ne),
        in_specs=[
            pl.BlockSpec((d, t), lambda ti, ei: (0, ti)),
            pl.BlockSpec((ec, d), lambda ti, ei: (ei, 0)),
            pl.BlockSpec((nsub, d, SUB_EXPERTS), lambda ti, ei: (ei, 0, 0)),
            tile4, tile4, tile4p, tile4p,
            pl.BlockSpec((t, d), lambda ti, ei: (ti, 0)),
            pl.BlockSpec((1, d), lambda ti, ei: (0, 0)),
        ],
        out_specs=pl.BlockSpec((t, d), lambda ti, ei: (ti, 0)),
        out_shape=jax.ShapeDtypeStruct((n, d), F32),
        scratch_shapes=[pltpu.VMEM((d, t), F32), pltpu.VMEM((nsub, SUB_EXPERTS, t), BF16),
                        pltpu.VMEM((2, PEER_HEADS, nlt // 2, ec // N_KEYS, PACK_ROWS, 2 * LANES), BF16)],
        compiler_params=pltpu.CompilerParams(
            dimension_semantics=("arbitrary", "arbitrary"), vmem_limit_bytes=VMEM_LIMIT),
    )(xnt, u, vt, c, e1, r2, e2, h1, fg)


def _rope_slab_tables(length):
    inv = 1.0 / (ROPE_THETA ** (jnp.arange(0, QK_ROPE, 2, dtype=F32) / QK_ROPE))
    ang = jnp.arange(length, dtype=F32)[:, None] * inv[None, :]
    cos, sin = jnp.cos(ang), jnp.sin(ang)
    half = QK_ROPE // 2
    pad = HEAD_SLAB - QK_NOPE - QK_ROPE
    ones = jnp.ones((length, QK_NOPE), F32)
    zeros_n = jnp.zeros((length, QK_NOPE), F32)
    zeros_p = jnp.zeros((length, pad), F32)
    del half
    cos_t = jnp.concatenate([ones, cos, cos, zeros_p], axis=1)
    sin_t = jnp.concatenate([zeros_n, -sin, sin, zeros_p], axis=1)
    return cos_t, sin_t


def _head_slabs(wcols, n_heads, width, offset=0):
    k = wcols.shape[0]
    w3 = wcols.reshape(k, n_heads, width)

    def at(off):
        return jnp.pad(w3, ((0, 0), (0, 0), (off, HEAD_SLAB - width - off)))

    if isinstance(offset, int):
        return at(offset).reshape(k, n_heads * HEAD_SLAB)
    out = None
    for off in sorted(set(offset)):
        pick = jnp.asarray([o == off for o in offset])[None, :, None]
        out = jnp.where(pick, at(off), 0.0) if out is None else jnp.where(pick, at(off), out)
    return out.reshape(k, n_heads * HEAD_SLAB)


def _prep_weights(norm1_g, w_in, q_norm_g, w_uq, kv_norm_g, w_ukv, conv_w, conv_b, attn_out_g,
                  conv_out_g, w_out, norm2_g, peer_w_query, peer_keys1, peer_keys2):
    half = QK_ROPE // 2
    o1 = Q_LORA
    o2 = o1 + KV_LORA
    o3 = o2 + QK_ROPE
    k_in = w_in.shape[0]
    kr = w_in[:, o2:o3]
    zpad_n = jnp.zeros((k_in, QK_NOPE), F32)
    zpad_p = jnp.zeros((k_in, HEAD_SLAB - QK_NOPE - QK_ROPE), F32)
    kr_a = jnp.concatenate([zpad_n, kr[:, :half], kr[:, half:], zpad_p], axis=1)
    kr_b = jnp.concatenate([zpad_n, kr[:, half:], kr[:, :half], zpad_p], axis=1)
    w_in_ext = jnp.concatenate([w_in[:, :o2], kr_a, kr_b, w_in[:, o3:]], axis=1).astype(BF16)

    qd = QK_NOPE + QK_ROPE
    wq3 = w_uq.reshape(Q_LORA, ATT_HEADS, qd)
    wq_swap = jnp.concatenate([wq3[:, :, :QK_NOPE], wq3[:, :, QK_NOPE + half:], wq3[:, :, QK_NOPE:QK_NOPE + half]],
                              axis=2).reshape(Q_LORA, ATT_HEADS * qd)
    w_q = jnp.concatenate([_head_slabs(w_uq, ATT_HEADS, qd), _head_slabs(wq_swap, ATT_HEADS, qd)],
                          axis=1).astype(BF16)

    wkv3 = w_ukv.reshape(KV_LORA, ATT_HEADS, QK_NOPE + V_HEAD)
    wk = wkv3[:, :, :QK_NOPE].reshape(KV_LORA, ATT_HEADS * QK_NOPE)
    wv = wkv3[:, :, QK_NOPE:].reshape(KV_LORA, ATT_HEADS * V_HEAD)
    v_off = [_v_lane(h) for h in range(ATT_HEADS)]
    w_kv = jnp.concatenate([_head_slabs(wk, ATT_HEADS, QK_NOPE), _head_slabs(wv, ATT_HEADS, V_HEAD, v_off)],
                           axis=1).astype(BF16)
    one_cols = np.zeros((1, ATT_HEADS * HEAD_SLAB), np.float32)
    for h in range(ATT_HEADS):
        one_cols[0, h * HEAD_SLAB + _one_lane(h)] = 1.0
    return {
        'v_one': jnp.asarray(one_cols),
        'norm1_g': norm1_g.reshape(1, -1), 'w_in': w_in_ext,
        'q_norm_g': q_norm_g.reshape(1, -1), 'w_q': w_q,
        'kv_norm_g': kv_norm_g.reshape(1, -1), 'w_kv': w_kv,
        'conv_w': conv_w, 'conv_b': conv_b.reshape(1, -1), 'conv_out_g': conv_out_g.reshape(1, -1),
        'attn_out_g': attn_out_g.reshape(1, -1), 'w_out': w_out.astype(BF16),
        'norm2_g': norm2_g.reshape(1, -1), 'w_query': peer_w_query.astype(BF16),
        'keys1': peer_keys1.astype(BF16), 'keys2': peer_keys2.astype(BF16),
    }


def kernel(x, meta_tokens, norm1_g, w_in, q_norm_g, w_uq, kv_norm_g, w_ukv, conv_w, conv_b, attn_out_g,
           conv_out_g, w_out, norm2_g, peer_w_query, peer_keys1, peer_keys2, peer_u, peer_v, final_g):
    b, s, d = x.shape
    assert norm1_g.shape[0] == 1, "single-layer kernel"
    assert meta_tokens.shape[0] == N_META
    tm = min(TM_PROJ, s)
    assert s % tm == 0 and s % LANES == 0
    w = _prep_weights(norm1_g[0], w_in[0], q_norm_g[0], w_uq[0], kv_norm_g[0], w_ukv[0], conv_w[0],
                      conv_b[0], attn_out_g[0], conv_out_g[0], w_out[0], norm2_g[0], peer_w_query[0],
                      peer_keys1[0], peer_keys2[0])
    d_conv = conv_b.shape[-1]
    cos_t, sin_t = _rope_slab_tables(N_META + s)

    _, k_m, v_m, _, z_m = _proj_call(meta_tokens[None], jnp.zeros((8, d_conv), F32), w,
                                     cos_t[:N_META], sin_t[:N_META], N_META)
    q, k, v, cn, _ = _proj_call(x, z_m[0], w, cos_t[N_META:], sin_t[N_META:], tm)
    hw = ATT_HEADS * HEAD_SLAB
    km = jnp.pad(k_m[0], ((0, LANES - N_META), (0, 0)))
    vm = jnp.pad(v_m[0], ((0, LANES - N_META), (0, 0)))
    att = _attn_call(q, k, v, km, vm, min(TQ_ATT, s), HEADS_PER_STEP)

    n = b * s
    tmix = min(TM_MIX, n)
    h1, xnt, c, e1, r2, e2 = _mix_call(att.reshape(n, D_ATT), cn.reshape(n, d_conv), x.reshape(n, d), w, tmix)
    u_bf = peer_u[0].astype(BF16)
    vt_bf = peer_v[0].reshape(-1, SUB_EXPERTS, d).transpose(0, 2, 1).astype(BF16)
    y = _peer_call(xnt, u_bf, vt_bf, c, e1, r2, e2, h1, final_g.reshape(1, -1), min(T_PEER, n), E_CHUNK)
    return y.reshape(b, s, d)
```

```python
import functools

import numpy as np
import jax
import jax.numpy as jnp
from jax import lax
from jax.experimental import pallas as pl
from jax.experimental.pallas import tpu as pltpu

F32 = jnp.float32
BF16 = jnp.bfloat16

N_META = 16
EPS = 1e-6
ROPE_THETA = 10000.0
ATT_HEADS = 8
Q_LORA = 256
KV_LORA = 128
QK_NOPE = 64
QK_ROPE = 32
V_HEAD = 64
D_ATT = ATT_HEADS * V_HEAD
PEER_HEADS = 8
N_KEYS = 128
D_KEY = 256
PEER_TOPK = 16
CONV_WIDTH = 3

LANES = 128
SUBLANES = 8
PACK_ROWS = 16
HEAD_SLAB = LANES
NEG_BIG = -1e30
LOG2E = 1.4426950408889634
VMEM_LIMIT = 56 * 1024 * 1024

TM_PROJ = 512
TQ_ATT = 512
TK_ATT = 512
HEADS_PER_STEP = 8
TM_MIX = 512
T_PEER = 512
E_CHUNK = 2048
SUB_EXPERTS = 2048
A_ROWS = 512


def _dot(a, b):
    return jnp.dot(a, b, preferred_element_type=F32)


def _dot_nt(a, b):
    return lax.dot_general(a, b, (((1,), (1,)), ((), ())), preferred_element_type=F32)


def _rms(x, g):
    y = x * lax.rsqrt(jnp.mean(x * x, axis=-1, keepdims=True) + EPS)
    return y * g


def _proj_kernel(x_ref, zprev_ref, g1_ref, win_ref, qg_ref, wq_ref, kvg_ref, wkv_ref, cw_ref, cb_ref,
                 cg_ref, cos_ref, sin_ref, vone_ref,
                 q_ref, k_ref, v_ref, cn_ref, zt_ref, zbuf, *, tm, d_conv, q_scale):
    t = pl.program_id(1)

    @pl.when(t == 0)
    def _():
        zbuf[0:8, :] = zprev_ref[...]

    x = x_ref[0]
    hn = _rms(x, g1_ref[...]).astype(BF16)
    p = _dot(hn, win_ref[...])
    cos = cos_ref[...]
    sin = sin_ref[...]

    qn = _rms(p[:, :Q_LORA], qg_ref[...]).astype(BF16)
    qq = _dot(qn, wq_ref[...])
    hw = ATT_HEADS * HEAD_SLAB
    for h in range(ATT_HEADS):
        lo = h * HEAD_SLAB
        qh = qq[:, lo:lo + HEAD_SLAB] * cos + qq[:, hw + lo:hw + lo + HEAD_SLAB] * sin
        q_ref[0, :, lo:lo + HEAD_SLAB] = (qh * q_scale).astype(BF16)

    o1 = Q_LORA
    o2 = o1 + KV_LORA
    kvn = _rms(p[:, o1:o2], kvg_ref[...]).astype(BF16)
    kv = _dot(kvn, wkv_ref[...])
    kr = p[:, o2:o2 + HEAD_SLAB] * cos + p[:, o2 + HEAD_SLAB:o2 + 2 * HEAD_SLAB] * sin
    for h in range(ATT_HEADS):
        lo = h * HEAD_SLAB
        k_ref[0, :, lo:lo + HEAD_SLAB] = (kv[:, lo:lo + HEAD_SLAB] + kr).astype(BF16)
    v_ref[0] = (kv[:, hw:] + vone_ref[...]).astype(BF16)

    o3 = o2 + 2 * HEAD_SLAB
    bg = p[:, o3:o3 + d_conv]
    z = p[:, o3 + d_conv:o3 + 2 * d_conv] * p[:, o3 + 2 * d_conv:o3 + 3 * d_conv]
    zbuf[8:8 + tm, :] = z
    conv = (cb_ref[...] + cw_ref[0:1, :] * zbuf[6:6 + tm, :] + cw_ref[1:2, :] * zbuf[7:7 + tm, :]
            + cw_ref[2:3, :] * z)
    cn_ref[0] = _rms(bg * conv, cg_ref[...]).astype(BF16)
    tail = zbuf[tm:tm + 8, :]
    zt_ref[0] = tail
    zbuf[0:8, :] = tail


def _proj_call(x, zprev, w, cos, sin, tm):
    b, s, d = x.shape
    d_conv = w['conv_b'].shape[-1]
    hw = ATT_HEADS * HEAD_SLAB
    nt = s // tm
    q_scale = float((QK_NOPE + QK_ROPE) ** -0.5 * LOG2E)
    const = lambda bi, ti: (0, 0)
    kern = functools.partial(_proj_kernel, tm=tm, d_conv=d_conv, q_scale=q_scale)
    return pl.pallas_call(
        kern,
        name="proj",
        grid=(b, nt),
        in_specs=[
            pl.BlockSpec((1, tm, d), lambda bi, ti: (bi, ti, 0)),
            pl.BlockSpec((8, d_conv), const),
            pl.BlockSpec((1, d), const),
            pl.BlockSpec(w['w_in'].shape, const),
            pl.BlockSpec((1, Q_LORA), const),
            pl.BlockSpec(w['w_q'].shape, const),
            pl.BlockSpec((1, KV_LORA), const),
            pl.BlockSpec(w['w_kv'].shape, const),
            pl.BlockSpec((CONV_WIDTH, d_conv), const),
            pl.BlockSpec((1, d_conv), const),
            pl.BlockSpec((1, d_conv), const),
            pl.BlockSpec((tm, HEAD_SLAB), lambda bi, ti: (ti, 0)),
            pl.BlockSpec((tm, HEAD_SLAB), lambda bi, ti: (ti, 0)),
            pl.BlockSpec((1, hw), const),
        ],
        out_specs=[
            pl.BlockSpec((1, tm, hw), lambda bi, ti: (bi, ti, 0)),
            pl.BlockSpec((1, tm, hw), lambda bi, ti: (bi, ti, 0)),
            pl.BlockSpec((1, tm, hw), lambda bi, ti: (bi, ti, 0)),
            pl.BlockSpec((1, tm, d_conv), lambda bi, ti: (bi, ti, 0)),
            pl.BlockSpec((1, 8, d_conv), lambda bi, ti: (bi, ti, 0)),
        ],
        out_shape=[
            jax.ShapeDtypeStruct((b, s, hw), BF16),
            jax.ShapeDtypeStruct((b, s, hw), BF16),
            jax.ShapeDtypeStruct((b, s, hw), BF16),
            jax.ShapeDtypeStruct((b, s, d_conv), BF16),
            jax.ShapeDtypeStruct((b, nt * 8, d_conv), F32),
        ],
        scratch_shapes=[pltpu.VMEM((tm + 8, d_conv), F32)],
        compiler_params=pltpu.CompilerParams(
            dimension_semantics=("arbitrary", "arbitrary"), vmem_limit_bytes=VMEM_LIMIT),
    )(x, zprev, w['norm1_g'], w['w_in'], w['q_norm_g'], w['w_q'], w['kv_norm_g'], w['w_kv'],
      w['conv_w'], w['conv_b'], w['conv_out_g'], cos, sin, w['v_one'])


def _v_lane(h):
    return V_HEAD * (h % 2)


def _one_lane(h):
    return V_HEAD * ((h + 1) % 2)


def _attn_kernel(q_ref, k_ref, v_ref, km_ref, vm_ref, o_ref, *, tq, tk, hg):
    i = pl.program_id(2)
    nd = tq // tk
    row = lax.broadcasted_iota(jnp.int32, (tq, tk), 0)
    col = lax.broadcasted_iota(jnp.int32, (tq, tk), 1)
    meta_ok = lax.broadcasted_iota(jnp.int32, (tq, LANES), 1) < N_META

    def step(carry, qh, kc, vc, mask):
        m, acc = carry
        s = _dot_nt(qh, kc)
        if mask is not None:
            s = jnp.where(mask, s, NEG_BIG)
        m_new = jnp.maximum(m, jnp.max(s, axis=-1, keepdims=True))
        pr = jnp.exp2(s - m_new)
        acc = jnp.exp2(m - m_new) * acc + _dot(pr.astype(BF16), vc)
        return m_new, acc

    slabs = [slice(hh * HEAD_SLAB, (hh + 1) * HEAD_SLAB) for hh in range(hg)]
    qs = [q_ref[0, :, sl] for sl in slabs]
    init = (jnp.full((tq, 1), NEG_BIG, F32), jnp.zeros((tq, LANES), F32))
    carries = tuple(step(init, qs[hh], km_ref[:, slabs[hh]], vm_ref[:, slabs[hh]], meta_ok)
                    for hh in range(hg))

    def body(j, cs):
        rows = pl.ds(pl.multiple_of(j * tk, tk), tk)
        return tuple(step(cs[hh], qs[hh], k_ref[0, rows, slabs[hh]], v_ref[0, rows, slabs[hh]], None)
                     for hh in range(hg))

    carries = lax.fori_loop(0, i * nd, body, carries)
    for c in range(nd):
        rows = pl.ds(pl.multiple_of(i * tq + c * tk, tk), tk)
        causal = col + c * tk <= row
        carries = tuple(step(carries[hh], qs[hh], k_ref[0, rows, slabs[hh]], v_ref[0, rows, slabs[hh]],
                             causal) for hh in range(hg))
    lane = lax.broadcasted_iota(jnp.int32, (tq, LANES), 1)
    outs = []
    for hh in range(hg):
        one = _one_lane(hh)
        acc = carries[hh][1]
        outs.append(jnp.where(lane == one, 0.0, acc / acc[:, one:one + 1]))
    o_ref[0] = jnp.concatenate([outs[2 * p] + outs[2 * p + 1] for p in range(hg // 2)], axis=-1).astype(BF16)


def _attn_call(q, k, v, km, vm, tq, hg):
    b, s, hw = q.shape
    gw = hg * HEAD_SLAB
    ow = (hg // 2) * LANES
    tk = min(TK_ATT, tq)
    kern = functools.partial(_attn_kernel, tq=tq, tk=tk, hg=hg)
    return pl.pallas_call(
        kern,
        name="attn",
        grid=(b, ATT_HEADS // hg, s // tq),
        in_specs=[
            pl.BlockSpec((1, tq, gw), lambda bi, gi, qi: (bi, qi, gi)),
            pl.BlockSpec((1, s, gw), lambda bi, gi, qi: (bi, 0, gi)),
            pl.BlockSpec((1, s, gw), lambda bi, gi, qi: (bi, 0, gi)),
            pl.BlockSpec((LANES, gw), lambda bi, gi, qi: (0, gi)),
            pl.BlockSpec((LANES, gw), lambda bi, gi, qi: (0, gi)),
        ],
        out_specs=pl.BlockSpec((1, tq, ow), lambda bi, gi, qi: (bi, qi, gi)),
        out_shape=jax.ShapeDtypeStruct((b, s, D_ATT), BF16),
        compiler_params=pltpu.CompilerParams(
            dimension_semantics=("arbitrary", "arbitrary", "arbitrary"), vmem_limit_bytes=VMEM_LIMIT),
    )(q, k, v, km, vm)


def _rank16_by_extraction(s):
    key = lax.broadcasted_iota(jnp.int32, s.shape, 0)
    rank = jnp.full(s.shape, float(PEER_TOPK), F32)
    for r in range(PEER_TOPK):
        m = jnp.max(s, axis=0, keepdims=True)
        first = jnp.min(jnp.where(s == m, key, N_KEYS), axis=0, keepdims=True)
        hit = key == first
        rank = jnp.where(hit, float(r), rank)
        s = jnp.where(hit, -jnp.inf, s)
    return rank


def _oddeven_merge(lo, hi, r):
    step = r * 2
    if step < hi - lo:
        yield from _oddeven_merge(lo, hi, step)
        yield from _oddeven_merge(lo + r, hi, step)
        yield from [(i, i + r) for i in range(lo + r, hi - r, step)]
    else:
        yield (lo, lo + r)


def _oddeven_merge_sort(lo, hi):
    if hi - lo >= 1:
        mid = lo + (hi - lo) // 2
        yield from _oddeven_merge_sort(lo, mid)
        yield from _oddeven_merge_sort(mid + 1, hi)
        yield from _oddeven_merge(lo, hi, 1)


_SORT16_NET = tuple(_oddeven_merge_sort(0, PEER_TOPK - 1))


def _sorted_top16(tiles):
    cols = list(tiles) + [None] * (PEER_TOPK - len(tiles))

    def vmax(x, y):
        return y if x is None else x if y is None else jnp.maximum(x, y)

    def vmin(x, y):
        return None if x is None or y is None else jnp.minimum(x, y)

    def exchange(i, j):
        cols[i], cols[j] = vmax(cols[i], cols[j]), vmin(cols[i], cols[j])

    for i, j in _SORT16_NET:
        exchange(i, j)
    for shift in (4, 2, 1):
        partner = [None if c is None else pltpu.roll(c, shift, axis=0) for c in reversed(cols)]
        cols = [vmax(cols[r], partner[r]) for r in range(PEER_TOPK)]
        for stride in (8, 4, 2, 1):
            for r in range(PEER_TOPK):
                if r & stride == 0:
                    exchange(r, r + stride)
    return cols


def _top16_tile(s, want_rank):
    nv = N_KEYS // SUBLANES
    rows = [s[SUBLANES * v:SUBLANES * (v + 1), :] for v in range(nv)]
    top = _sorted_top16(rows)
    vals = jnp.concatenate([t[0:1, :] for t in top], axis=0)
    at_least = jnp.zeros((SUBLANES, LANES), F32)
    for v in range(nv):
        at_least = at_least + jnp.where(rows[v] >= top[PEER_TOPK - 1], 1.0, 0.0)
    tied = jnp.where(jnp.sum(at_least, axis=0, keepdims=True) != float(PEER_TOPK), 1.0, 0.0)
    for r in range(PEER_TOPK - 1):
        tied = tied + jnp.where(top[r][0:1, :] <= top[r + 1][0:1, :], 1.0, 0.0)
    if not want_rank:
        return vals, None, tied
    rank = []
    for v in range(nv):
        rv = jnp.zeros((SUBLANES, LANES), F32)
        for r in range(PEER_TOPK):
            rv = rv + jnp.where(top[r] > rows[v], 1.0, 0.0)
        rank.append(rv)
    return vals, jnp.concatenate(rank, axis=0), tied


def _candidate_blocks():
    blocks = []
    for a in range(8):
        nb = PEER_TOPK // (a + 1)
        for b0 in range(0, nb, 8):
            nv = min(8, nb - b0)
            flat = np.array([a * PEER_TOPK + b0 + r for r in range(8)])
            blocks.append(('row', a, b0, nv, flat))
    flat = np.array([(8 + r) * PEER_TOPK for r in range(8)])
    blocks.append(('col', 8, 0, 8, flat))
    return blocks


_CAND_BLOCKS = _candidate_blocks()


def _select_counts(v1, v2, exact):
    sub = lax.broadcasted_iota(jnp.int32, (8, LANES), 0)
    vals = []
    for kind, a0, b0, nv, _ in _CAND_BLOCKS:
        if kind == 'row':
            blk = v1[a0:a0 + 1, :] + v2[b0:b0 + 8, :]
        else:
            blk = v1[a0:a0 + 8, :] + v2[b0:b0 + 1, :]
        if nv < 8:
            blk = jnp.where(sub < nv, blk, -jnp.inf)
        vals.append(blk)
    def by_pairwise_rank(blocks):
        beaten = [jnp.zeros((8, LANES), F32) for _ in blocks]
        for si, (_, _, _, nv_s, flat_s) in enumerate(_CAND_BLOCKS):
            for r in range(nv_s):
                src = jnp.broadcast_to(blocks[si][r:r + 1, :], (8, LANES))
                for ti, (_, _, _, _, flat_t) in enumerate(_CAND_BLOCKS):
                    wins_tie = flat_s[r] < flat_t
                    gt = jnp.where(src > blocks[ti], 1.0, 0.0)
                    ge = jnp.where(src >= blocks[ti], 1.0, 0.0)
                    if wins_tie.all():
                        beats = ge
                    elif not wins_tie.any():
                        beats = gt
                    else:
                        beats = jnp.where(sub >= int((~wins_tie).sum()), ge, gt)
                    beaten[ti] = beaten[ti] + beats
        return tuple(jnp.where(b < float(PEER_TOPK), 1.0, 0.0) for b in beaten)

    if exact:
        selected = by_pairwise_rank(vals)
        tied = jnp.zeros((1, LANES), F32)
    else:
        kth = _sorted_top16(vals)[PEER_TOPK - 1]
        selected = tuple(jnp.where(blk >= kth, 1.0, 0.0) for blk in vals)
        n_reach = jnp.sum(sum(selected[1:], selected[0]), axis=0, keepdims=True)
        tied = jnp.where(n_reach != float(PEER_TOPK), 1.0, 0.0)

    top = vals[0][0:1, :]
    z = jnp.zeros((1, LANES), F32)
    cnt_rows = [jnp.zeros((1, LANES), F32) for _ in range(PEER_TOPK)]
    for ti, (kind, a0, b0, nv, _) in enumerate(_CAND_BLOCKS):
        self_f = selected[ti]
        if nv < 8:
            self_f = jnp.where(sub < nv, self_f, 0.0)
        z = z + jnp.sum(self_f * jnp.exp(vals[ti] - top), axis=0, keepdims=True)
        if kind == 'row':
            cnt_rows[a0] = cnt_rows[a0] + jnp.sum(self_f, axis=0, keepdims=True)
        else:
            for r in range(8):
                cnt_rows[a0 + r] = cnt_rows[a0 + r] + self_f[r:r + 1, :]
    return jnp.concatenate(cnt_rows, axis=0), z, tied


def _mix_kernel(att_ref, cn_ref, x_ref, ag_ref, wo_ref, g2_ref, wqr_ref, k1_ref, k2_ref,
                h1_ref, xnt_ref, c_ref, e1_ref, r2_ref, e2_ref, st, rk, vs, *, tm):
    nlt = tm // LANES
    att = _rms(att_ref[...].astype(F32), ag_ref[...]).astype(BF16)
    d_att = att.shape[-1]
    h1 = x_ref[...] + _dot(att, wo_ref[0:d_att, :]) + _dot(cn_ref[...], wo_ref[d_att:, :])
    h1_ref[...] = h1
    xn32 = _rms(h1, g2_ref[...])
    xn = xn32.astype(BF16)
    xnt_ref[...] = xn32.T.astype(BF16)
    qp = _dot(xn, wqr_ref[...]).astype(BF16)
    half = D_KEY // 2
    for h in range(PEER_HEADS):
        for side, kref in enumerate((k1_ref, k2_ref)):
            lo = h * D_KEY + side * half
            sc = _dot_nt(kref[...], qp[:, lo:lo + half])
            for lt in range(nlt):
                st[2 * h + side, lt] = sc[:, lt * LANES:(lt + 1) * LANES]

    no_tie = jnp.zeros((1, LANES), F32)

    def rank_body(idx, tied):
        h = idx // nlt
        lt = idx % nlt
        vals1, _, t1 = _top16_tile(st[2 * h, lt], False)
        vals2, rank2, t2 = _top16_tile(st[2 * h + 1, lt], True)
        vs[2 * h, lt] = vals1
        vs[2 * h + 1, lt] = vals2
        rk[2 * h + 1, lt] = rank2
        return jnp.maximum(tied, jnp.maximum(t1, t2))

    def rank_exact_body(idx, carry):
        p = idx // nlt
        lt = idx % nlt
        rk[p, lt] = _rank16_by_extraction(st[p, lt])
        return carry

    def sel_body(exact, idx, tied):
        h = idx // (nlt // 2)
        lt2 = idx % (nlt // 2)
        for half in range(2):
            lt = 2 * lt2 + half
            ls = slice(half * LANES, (half + 1) * LANES)
            v1 = vs[2 * h, lt]
            v2 = vs[2 * h + 1, lt]
            s1 = st[2 * h, lt]
            cnt, z, t = _select_counts(v1, v2, exact)
            tied = jnp.maximum(tied, t)
            r1 = rk[2 * h, lt] if exact else None
            c = jnp.zeros((N_KEYS, LANES), F32)
            for a in range(PEER_TOPK):
                hit = (r1 == float(a)) if exact else (s1 == v1[a:a + 1, :])
                c = c + jnp.where(hit, cnt[a:a + 1, :], 0.0)
            c_ref[h, lt] = c
            e1_ref[h, lt] = jnp.exp(s1 - v1[0:1, :]) * (1.0 / z)
            r2_ref[h, lt2, :, ls] = rk[2 * h + 1, lt].astype(BF16)
            e2_ref[h, lt2, :, ls] = jnp.exp(st[2 * h + 1, lt] - v2[0:1, :]).astype(BF16)
        return tied

    tied = lax.fori_loop(0, PEER_HEADS * nlt, rank_body, no_tie)
    tied = lax.fori_loop(0, PEER_HEADS * (nlt // 2), functools.partial(sel_body, False), tied)

    @pl.when(jnp.max(tied) > 0.0)
    def _():
        lax.fori_loop(0, 2 * PEER_HEADS * nlt, rank_exact_body, 0)
        lax.fori_loop(0, PEER_HEADS * (nlt // 2), functools.partial(sel_body, True), no_tie)


def _mix_call(att, cn, x2, w, tm):
    n, d = x2.shape
    nlt = tm // LANES
    d_mix = w['w_out'].shape[0]
    const = lambda ti: (0, 0)
    tile4 = pl.BlockSpec((PEER_HEADS, nlt, N_KEYS, LANES), lambda ti: (0, ti, 0, 0))
    tile4p = pl.BlockSpec((PEER_HEADS, nlt // 2, N_KEYS, 2 * LANES), lambda ti: (0, ti, 0, 0))
    sel_shape = (PEER_HEADS, n // LANES, N_KEYS, LANES)
    selp_shape = (PEER_HEADS, n // (2 * LANES), N_KEYS, 2 * LANES)
    kern = functools.partial(_mix_kernel, tm=tm)
    return pl.pallas_call(
        kern,
        name="mix",
        grid=(n // tm,),
        in_specs=[
            pl.BlockSpec((tm, att.shape[-1]), lambda ti: (ti, 0)),
            pl.BlockSpec((tm, cn.shape[-1]), lambda ti: (ti, 0)),
            pl.BlockSpec((tm, d), lambda ti: (ti, 0)),
            pl.BlockSpec((1, att.shape[-1]), const),
            pl.BlockSpec((d_mix, d), const),
            pl.BlockSpec((1, d), const),
            pl.BlockSpec(w['w_query'].shape, const),
            pl.BlockSpec((N_KEYS, D_KEY // 2), const),
            pl.BlockSpec((N_KEYS, D_KEY // 2), const),
        ],
        out_specs=[
            pl.BlockSpec((tm, d), lambda ti: (ti, 0)),
            pl.BlockSpec((d, tm), lambda ti: (0, ti)),
            tile4, tile4, tile4p, tile4p,
        ],
        out_shape=[
            jax.ShapeDtypeStruct((n, d), F32),
            jax.ShapeDtypeStruct((d, n), BF16),
            jax.ShapeDtypeStruct(sel_shape, F32),
            jax.ShapeDtypeStruct(sel_shape, F32),
            jax.ShapeDtypeStruct(selp_shape, BF16),
            jax.ShapeDtypeStruct(selp_shape, BF16),
        ],
        scratch_shapes=[
            pltpu.VMEM((2 * PEER_HEADS, nlt, N_KEYS, LANES), F32),
            pltpu.VMEM((2 * PEER_HEADS, nlt, N_KEYS, LANES), F32),
            pltpu.VMEM((2 * PEER_HEADS, nlt, PEER_TOPK, LANES), F32),
        ],
        compiler_params=pltpu.CompilerParams(
            dimension_semantics=("arbitrary",), vmem_limit_bytes=VMEM_LIMIT),
    )(att, cn, x2, w['attn_out_g'], w['w_out'], w['norm2_g'], w['w_query'], w['keys1'], w['keys2'])


def _bf16_split(value):
    hi = float(np.asarray(value, dtype=BF16))
    lo = float(np.asarray(value - hi, dtype=BF16))
    return hi, lo


def _gelu_tanh(x):
    c = float(np.sqrt(2.0 / np.pi))
    c_hi, c_lo = _bf16_split(c)
    k_hi, k_lo = _bf16_split(c * 0.044715)
    x2 = x * x
    poly = (x2 * k_hi + c_hi) + (x2 * k_lo + c_lo)
    hx = 0.5 * x
    return hx + hx * jnp.tanh(x * poly)


def _peer_kernel(xnt_ref, u_ref, vt_ref, c_ref, e1_ref, r2_ref, e2_ref, h1_ref, fg_ref, y_ref,
                 acc, pbuf, tbl, *, t, ec):
    e = pl.program_id(1)
    nlt = t // LANES
    nib = ec // N_KEYS
    nsub = ec // SUB_EXPERTS
    kps = SUB_EXPERTS // N_KEYS
    npos = N_KEYS // PACK_ROWS

    @pl.when(e == 0)
    def _():
        acc[...] = jnp.zeros_like(acc)

    row0 = pl.multiple_of(e * nib, nib)
    for h in range(PEER_HEADS):
        for lt in range(nlt):
            ls = slice((lt % 2) * LANES, (lt % 2 + 1) * LANES)
            c8 = c_ref[h, lt, pl.ds(row0, nib), :]
            e8 = e1_ref[h, lt, pl.ds(row0, nib), :]
            for kk in range(nib):
                tbl[0, h, lt // 2, kk, :, ls] = jnp.broadcast_to(c8[kk:kk + 1, :], (PACK_ROWS, LANES)).astype(BF16)
                tbl[1, h, lt // 2, kk, :, ls] = jnp.broadcast_to(e8[kk:kk + 1, :], (PACK_ROWS, LANES)).astype(BF16)

    gates = {}
    for k in range(nib):
        for lt2 in range(nlt // 2):
            gate = None
            for h in range(PEER_HEADS):
                cb = jnp.concatenate([tbl[0, h, lt2, k]] * npos, axis=0)
                eb = jnp.concatenate([tbl[1, h, lt2, k]] * npos, axis=0)
                term = jnp.where(r2_ref[h, lt2] < cb, e2_ref[h, lt2] * eb, jnp.zeros((), BF16))
                gate = term if gate is None else gate + term
            gates[k, lt2] = gate
    kpa = A_ROWS // N_KEYS
    d_model = u_ref.shape[1]
    for j in range(ec // A_ROWS):
        lhs = u_ref[j * A_ROWS:(j + 1) * A_ROWS, :]
        if j > 0:
            anchor = gates[(j - 1) * kpa, 0][0:PACK_ROWS, 0:LANES]
            zero = jnp.minimum(jnp.abs(anchor), jnp.zeros((), BF16))
            zero = jnp.concatenate([jnp.concatenate([zero] * (d_model // LANES), axis=1)] * (A_ROWS // PACK_ROWS),
                                   axis=0)
            lhs = lhs + zero
        act = _dot(lhs, xnt_ref[...])
        for kk in range(kpa):
            k = j * kpa + kk
            for lt2 in range(nlt // 2):
                ls = slice(lt2 * 2 * LANES, (lt2 + 1) * 2 * LANES)
                gl = _gelu_tanh(act[kk * N_KEYS:(kk + 1) * N_KEYS, ls].astype(BF16))
                pbuf[0, k * N_KEYS:(k + 1) * N_KEYS, ls] = gates[k, lt2] * gl
    acc[...] += _dot(vt_ref[0], pbuf[0])

    @pl.when(e == pl.num_programs(1) - 1)
    def _():
        h2 = h1_ref[...] + acc[...].T
        y_ref[...] = _rms(h2, fg_ref[...])


def _peer_call(xnt, u, vt, c, e1, r2, e2, h1, fg, t, ec):
    d, n = xnt.shape
    ne = u.shape[0] // ec
    nsub = ec // SUB_EXPERTS
    assert vt.shape == (ne * nsub, d, SUB_EXPERTS)
    nlt = t // LANES
    tile4 = pl.BlockSpec((PEER_HEADS, nlt, N_KEYS, LANES), lambda ti, ei: (0, ti, 0, 0))
    tile4p = pl.BlockSpec((PEER_HEADS, nlt // 2, N_KEYS, 2 * LANES), lambda ti, ei: (0, ti, 0, 0))
    kern = functools.partial(_peer_kernel, t=t, ec=ec)
    return pl.pallas_call(
        kern,
        name="peer",
        grid=(n // t, ne),
        in_specs=[
            pl.BlockSpec((d, t), lambda ti, ei: (0, ti)),
            pl.BlockSpec((ec, d), lambda ti, ei: (ei, 0)),
            pl.BlockSpec((nsub, d, SUB_EXPERTS), lambda ti, ei: (ei, 0, 0)),
            tile4, tile4, tile4p, tile4p,
            pl.BlockSpec((t, d), lambda ti, ei: (ti, 0)),
            pl.BlockSpec((1, d), lambda ti, ei: (0, 0)),
        ],
        out_specs=pl.BlockSpec((t, d), lambda ti, ei: (ti, 0)),
        out_shape=jax.ShapeDtypeStruct((n, d), F32),
        scratch_shapes=[pltpu.VMEM((d, t), F32), pltpu.VMEM((nsub, SUB_EXPERTS, t), BF16),
                        pltpu.VMEM((2, PEER_HEADS, nlt // 2, ec // N_KEYS, PACK_ROWS, 2 * LANES), BF16)],
        compiler_params=pltpu.CompilerParams(
            dimension_semantics=("arbitrary", "arbitrary"), vmem_limit_bytes=VMEM_LIMIT),
    )(xnt, u, vt, c, e1, r2, e2, h1, fg)


def _rope_slab_tables(length):
    inv = 1.0 / (ROPE_THETA ** (jnp.arange(0, QK_ROPE, 2, dtype=F32) / QK_ROPE))
    ang = jnp.arange(length, dtype=F32)[:, None] * inv[None, :]
    cos, sin = jnp.cos(ang), jnp.sin(ang)
    half = QK_ROPE // 2
    pad = HEAD_SLAB - QK_NOPE - QK_ROPE
    ones = jnp.ones((length, QK_NOPE), F32)
    zeros_n = jnp.zeros((length, QK_NOPE), F32)
    zeros_p = jnp.zeros((length, pad), F32)
    del half
    cos_t = jnp.concatenate([ones, cos, cos, zeros_p], axis=1)
    sin_t = jnp.concatenate([zeros_n, -sin, sin, zeros_p], axis=1)
    return cos_t, sin_t


def _head_slabs(wcols, n_heads, width, offset=0):
    k = wcols.shape[0]
    w3 = wcols.reshape(k, n_heads, width)

    def at(off):
        return jnp.pad(w3, ((0, 0), (0, 0), (off, HEAD_SLAB - width - off)))

    if isinstance(offset, int):
        return at(offset).reshape(k, n_heads * HEAD_SLAB)
    out = None
    for off in sorted(set(offset)):
        pick = jnp.asarray([o == off for o in offset])[None, :, None]
        out = jnp.where(pick, at(off), 0.0) if out is None else jnp.where(pick, at(off), out)
    return out.reshape(k, n_heads * HEAD_SLAB)


def _prep_weights(norm1_g, w_in, q_norm_g, w_uq, kv_norm_g, w_ukv, conv_w, conv_b, attn_out_g,
                  conv_out_g, w_out, norm2_g, peer_w_query, peer_keys1, peer_keys2):
    half = QK_ROPE // 2
    o1 = Q_LORA
    o2 = o1 + KV_LORA
    o3 = o2 + QK_ROPE
    k_in = w_in.shape[0]
    kr = w_in[:, o2:o3]
    zpad_n = jnp.zeros((k_in, QK_NOPE), F32)
    zpad_p = jnp.zeros((k_in, HEAD_SLAB - QK_NOPE - QK_ROPE), F32)
    kr_a = jnp.concatenate([zpad_n, kr[:, :half], kr[:, half:], zpad_p], axis=1)
    kr_b = jnp.concatenate([zpad_n, kr[:, half:], kr[:, :half], zpad_p], axis=1)
    w_in_ext = jnp.concatenate([w_in[:, :o2], kr_a, kr_b, w_in[:, o3:]], axis=1).astype(BF16)

    qd = QK_NOPE + QK_ROPE
    wq3 = w_uq.reshape(Q_LORA, ATT_HEADS, qd)
    wq_swap = jnp.concatenate([wq3[:, :, :QK_NOPE], wq3[:, :, QK_NOPE + half:], wq3[:, :, QK_NOPE:QK_NOPE + half]],
                              axis=2).reshape(Q_LORA, ATT_HEADS * qd)
    w_q = jnp.concatenate([_head_slabs(w_uq, ATT_HEADS, qd), _head_slabs(wq_swap, ATT_HEADS, qd)],
                          axis=1).astype(BF16)

    wkv3 = w_ukv.reshape(KV_LORA, ATT_HEADS, QK_NOPE + V_HEAD)
    wk = wkv3[:, :, :QK_NOPE].reshape(KV_LORA, ATT_HEADS * QK_NOPE)
    wv = wkv3[:, :, QK_NOPE:].reshape(KV_LORA, ATT_HEADS * V_HEAD)
    v_off = [_v_lane(h) for h in range(ATT_HEADS)]
    w_kv = jnp.concatenate([_head_slabs(wk, ATT_HEADS, QK_NOPE), _head_slabs(wv, ATT_HEADS, V_HEAD, v_off)],
                           axis=1).astype(BF16)
    one_cols = np.zeros((1, ATT_HEADS * HEAD_SLAB), np.float32)
    for h in range(ATT_HEADS):
        one_cols[0, h * HEAD_SLAB + _one_lane(h)] = 1.0
    return {
        'v_one': jnp.asarray(one_cols),
        'norm1_g': norm1_g.reshape(1, -1), 'w_in': w_in_ext,
        'q_norm_g': q_norm_g.reshape(1, -1), 'w_q': w_q,
        'kv_norm_g': kv_norm_g.reshape(1, -1), 'w_kv': w_kv,
        'conv_w': conv_w, 'conv_b': conv_b.reshape(1, -1), 'conv_out_g': conv_out_g.reshape(1, -1),
        'attn_out_g': attn_out_g.reshape(1, -1), 'w_out': w_out.astype(BF16),
        'norm2_g': norm2_g.reshape(1, -1), 'w_query': peer_w_query.astype(BF16),
        'keys1': peer_keys1.astype(BF16), 'keys2': peer_keys2.astype(BF16),
    }


def kernel(x, meta_tokens, norm1_g, w_in, q_norm_g, w_uq, kv_norm_g, w_ukv, conv_w, conv_b, attn_out_g,
           conv_out_g, w_out, norm2_g, peer_w_query, peer_keys1, peer_keys2, peer_u, peer_v, final_g):
    b, s, d = x.shape
    assert norm1_g.shape[0] == 1, "single-layer kernel"
    assert meta_tokens.shape[0] == N_META
    tm = min(TM_PROJ, s)
    assert s % tm == 0 and s % LANES == 0
    w = _prep_weights(norm1_g[0], w_in[0], q_norm_g[0], w_uq[0], kv_norm_g[0], w_ukv[0], conv_w[0],
                      conv_b[0], attn_out_g[0], conv_out_g[0], w_out[0], norm2_g[0], peer_w_query[0],
                      peer_keys1[0], peer_keys2[0])
    d_conv = conv_b.shape[-1]
    cos_t, sin_t = _rope_slab_tables(N_META + s)

    _, k_m, v_m, _, z_m = _proj_call(meta_tokens[None], jnp.zeros((8, d_conv), F32), w,
                                     cos_t[:N_META], sin_t[:N_META], N_META)
    q, k, v, cn, _ = _proj_call(x, z_m[0], w, cos_t[N_META:], sin_t[N_META:], tm)
    hw = ATT_HEADS * HEAD_SLAB
    km = jnp.pad(k_m[0], ((0, LANES - N_META), (0, 0)))
    vm = jnp.pad(v_m[0], ((0, LANES - N_META), (0, 0)))
    att = _attn_call(q, k, v, km, vm, min(TQ_ATT, s), HEADS_PER_STEP)

    n = b * s
    tmix = min(TM_MIX, n)
    h1, xnt, c, e1, r2, e2 = _mix_call(att.reshape(n, D_ATT), cn.reshape(n, d_conv), x.reshape(n, d), w, tmix)
    u_bf = peer_u[0].astype(BF16)
    vt_bf = peer_v[0].reshape(-1, SUB_EXPERTS, d).transpose(0, 2, 1).astype(BF16)
    y = _peer_call(xnt, u_bf, vt_bf, c, e1, r2, e2, h1, final_g.reshape(1, -1), min(T_PEER, n), E_CHUNK)
    return y.reshape(b, s, d)
```

```python
import functools

import numpy as np
import jax
import jax.numpy as jnp
from jax import lax
from jax.experimental import pallas as pl
from jax.experimental.pallas import tpu as pltpu

F32 = jnp.float32
BF16 = jnp.bfloat16

N_META = 16
EPS = 1e-6
ROPE_THETA = 10000.0
ATT_HEADS = 8
Q_LORA = 256
KV_LORA = 128
QK_NOPE = 64
QK_ROPE = 32
V_HEAD = 64
D_ATT = ATT_HEADS * V_HEAD
PEER_HEADS = 8
N_KEYS = 128
D_KEY = 256
PEER_TOPK = 16
CONV_WIDTH = 3

LANES = 128
SUBLANES = 8
PACK_ROWS = 16
HEAD_SLAB = LANES
NEG_BIG = -1e30
LOG2E = 1.4426950408889634
VMEM_LIMIT = 56 * 1024 * 1024

TM_PROJ = 512
TQ_ATT = 512
TK_ATT = 512
HEADS_PER_STEP = 8
TM_MIX = 512
T_PEER = 512
E_CHUNK = 2048
SUB_EXPERTS = 2048
A_ROWS = 512


def _dot(a, b):
    return jnp.dot(a, b, preferred_element_type=F32)


def _dot_nt(a, b):
    return lax.dot_general(a, b, (((1,), (1,)), ((), ())), preferred_element_type=F32)


def _rms(x, g):
    y = x * lax.rsqrt(jnp.mean(x * x, axis=-1, keepdims=True) + EPS)
    return y * g


def _proj_kernel(x_ref, zprev_ref, g1_ref, win_ref, qg_ref, wq_ref, kvg_ref, wkv_ref, cw_ref, cb_ref,
                 cg_ref, cos_ref, sin_ref, vone_ref,
                 q_ref, k_ref, v_ref, cn_ref, zt_ref, zbuf, *, tm, d_conv, q_scale):
    t = pl.program_id(1)

    @pl.when(t == 0)
    def _():
        zbuf[0:8, :] = zprev_ref[...]

    x = x_ref[0]
    hn = _rms(x, g1_ref[...]).astype(BF16)
    p = _dot(hn, win_ref[...])
    cos = cos_ref[...]
    sin = sin_ref[...]

    qn = _rms(p[:, :Q_LORA], qg_ref[...]).astype(BF16)
    qq = _dot(qn, wq_ref[...])
    hw = ATT_HEADS * HEAD_SLAB
    for h in range(ATT_HEADS):
        lo = h * HEAD_SLAB
        qh = qq[:, lo:lo + HEAD_SLAB] * cos + qq[:, hw + lo:hw + lo + HEAD_SLAB] * sin
        q_ref[0, :, lo:lo + HEAD_SLAB] = (qh * q_scale).astype(BF16)

    o1 = Q_LORA
    o2 = o1 + KV_LORA
    kvn = _rms(p[:, o1:o2], kvg_ref[...]).astype(BF16)
    kv = _dot(kvn, wkv_ref[...])
    kr = p[:, o2:o2 + HEAD_SLAB] * cos + p[:, o2 + HEAD_SLAB:o2 + 2 * HEAD_SLAB] * sin
    for h in range(ATT_HEADS):
        lo = h * HEAD_SLAB
        k_ref[0, :, lo:lo + HEAD_SLAB] = (kv[:, lo:lo + HEAD_SLAB] + kr).astype(BF16)
    v_ref[0] = (kv[:, hw:] + vone_ref[...]).astype(BF16)

    o3 = o2 + 2 * HEAD_SLAB
    bg = p[:, o3:o3 + d_conv]
    z = p[:, o3 + d_conv:o3 + 2 * d_conv] * p[:, o3 + 2 * d_conv:o3 + 3 * d_conv]
    zbuf[8:8 + tm, :] = z
    conv = (cb_ref[...] + cw_ref[0:1, :] * zbuf[6:6 + tm, :] + cw_ref[1:2, :] * zbuf[7:7 + tm, :]
            + cw_ref[2:3, :] * z)
    cn_ref[0] = _rms(bg * conv, cg_ref[...]).astype(BF16)
    tail = zbuf[tm:tm + 8, :]
    zt_ref[0] = tail
    zbuf[0:8, :] = tail


def _proj_call(x, zprev, w, cos, sin, tm):
    b, s, d = x.shape
    d_conv = w['conv_b'].shape[-1]
    hw = ATT_HEADS * HEAD_SLAB
    nt = s // tm
    q_scale = float((QK_NOPE + QK_ROPE) ** -0.5 * LOG2E)
    const = lambda bi, ti: (0, 0)
    kern = functools.partial(_proj_kernel, tm=tm, d_conv=d_conv, q_scale=q_scale)
    return pl.pallas_call(
        kern,
        name="proj",
        grid=(b, nt),
        in_specs=[
            pl.BlockSpec((1, tm, d), lambda bi, ti: (bi, ti, 0)),
            pl.BlockSpec((8, d_conv), const),
            pl.BlockSpec((1, d), const),
            pl.BlockSpec(w['w_in'].shape, const),
            pl.BlockSpec((1, Q_LORA), const),
            pl.BlockSpec(w['w_q'].shape, const),
            pl.BlockSpec((1, KV_LORA), const),
            pl.BlockSpec(w['w_kv'].shape, const),
            pl.BlockSpec((CONV_WIDTH, d_conv), const),
            pl.BlockSpec((1, d_conv), const),
            pl.BlockSpec((1, d_conv), const),
            pl.BlockSpec((tm, HEAD_SLAB), lambda bi, ti: (ti, 0)),
            pl.BlockSpec((tm, HEAD_SLAB), lambda bi, ti: (ti, 0)),
            pl.BlockSpec((1, hw), const),
        ],
        out_specs=[
            pl.BlockSpec((1, tm, hw), lambda bi, ti: (bi, ti, 0)),
            pl.BlockSpec((1, tm, hw), lambda bi, ti: (bi, ti, 0)),
            pl.BlockSpec((1, tm, hw), lambda bi, ti: (bi, ti, 0)),
            pl.BlockSpec((1, tm, d_conv), lambda bi, ti: (bi, ti, 0)),
            pl.BlockSpec((1, 8, d_conv), lambda bi, ti: (bi, ti, 0)),
        ],
        out_shape=[
            jax.ShapeDtypeStruct((b, s, hw), BF16),
            jax.ShapeDtypeStruct((b, s, hw), BF16),
            jax.ShapeDtypeStruct((b, s, hw), BF16),
            jax.ShapeDtypeStruct((b, s, d_conv), BF16),
            jax.ShapeDtypeStruct((b, nt * 8, d_conv), F32),
        ],
        scratch_shapes=[pltpu.VMEM((tm + 8, d_conv), F32)],
        compiler_params=pltpu.CompilerParams(
            dimension_semantics=("arbitrary", "arbitrary"), vmem_limit_bytes=VMEM_LIMIT),
    )(x, zprev, w['norm1_g'], w['w_in'], w['q_norm_g'], w['w_q'], w['kv_norm_g'], w['w_kv'],
      w['conv_w'], w['conv_b'], w['conv_out_g'], cos, sin, w['v_one'])


def _v_lane(h):
    return V_HEAD * (h % 2)


def _one_lane(h):
    return V_HEAD * ((h + 1) % 2)


def _attn_kernel(q_ref, k_ref, v_ref, km_ref, vm_ref, o_ref, *, tq, tk, hg):
    i = pl.program_id(2)
    nd = tq // tk
    row = lax.broadcasted_iota(jnp.int32, (tq, tk), 0)
    col = lax.broadcasted_iota(jnp.int32, (tq, tk), 1)
    meta_ok = lax.broadcasted_iota(jnp.int32, (tq, LANES), 1) < N_META

    def step(carry, qh, kc, vc, mask):
        m, acc = carry
        s = _dot_nt(qh, kc)
        if mask is not None:
            s = jnp.where(mask, s, NEG_BIG)
        m_new = jnp.maximum(m, jnp.max(s, axis=-1, keepdims=True))
        pr = jnp.exp2(s - m_new)
        acc = jnp.exp2(m - m_new) * acc + _dot(pr.astype(BF16), vc)
        return m_new, acc

    slabs = [slice(hh * HEAD_SLAB, (hh + 1) * HEAD_SLAB) for hh in range(hg)]
    qs = [q_ref[0, :, sl] for sl in slabs]
    init = (jnp.full((tq, 1), NEG_BIG, F32), jnp.zeros((tq, LANES), F32))
    carries = tuple(step(init, qs[hh], km_ref[:, slabs[hh]], vm_ref[:, slabs[hh]], meta_ok)
                    for hh in range(hg))

    def body(j, cs):
        rows = pl.ds(pl.multiple_of(j * tk, tk), tk)
        return tuple(step(cs[hh], qs[hh], k_ref[0, rows, slabs[hh]], v_ref[0, rows, slabs[hh]], None)
                     for hh in range(hg))

    carries = lax.fori_loop(0, i * nd, body, carries)
    for c in range(nd):
        rows = pl.ds(pl.multiple_of(i * tq + c * tk, tk), tk)
        causal = col + c * tk <= row
        carries = tuple(step(carries[hh], qs[hh], k_ref[0, rows, slabs[hh]], v_ref[0, rows, slabs[hh]],
                             causal) for hh in range(hg))
    lane = lax.broadcasted_iota(jnp.int32, (tq, LANES), 1)
    outs = []
    for hh in range(hg):
        one = _one_lane(hh)
        acc = carries[hh][1]
        outs.append(jnp.where(lane == one, 0.0, acc / acc[:, one:one + 1]))
    o_ref[0] = jnp.concatenate([outs[2 * p] + outs[2 * p + 1] for p in range(hg // 2)], axis=-1).astype(BF16)


def _attn_call(q, k, v, km, vm, tq, hg):
    b, s, hw = q.shape
    gw = hg * HEAD_SLAB
    ow = (hg // 2) * LANES
    tk = min(TK_ATT, tq)
    kern = functools.partial(_attn_kernel, tq=tq, tk=tk, hg=hg)
    return pl.pallas_call(
        kern,
        name="attn",
        grid=(b, ATT_HEADS // hg, s // tq),
        in_specs=[
            pl.BlockSpec((1, tq, gw), lambda bi, gi, qi: (bi, qi, gi)),
            pl.BlockSpec((1, s, gw), lambda bi, gi, qi: (bi, 0, gi)),
            pl.BlockSpec((1, s, gw), lambda bi, gi, qi: (bi, 0, gi)),
            pl.BlockSpec((LANES, gw), lambda bi, gi, qi: (0, gi)),
            pl.BlockSpec((LANES, gw), lambda bi, gi, qi: (0, gi)),
        ],
        out_specs=pl.BlockSpec((1, tq, ow), lambda bi, gi, qi: (bi, qi, gi)),
        out_shape=jax.ShapeDtypeStruct((b, s, D_ATT), BF16),
        compiler_params=pltpu.CompilerParams(
            dimension_semantics=("arbitrary", "arbitrary", "arbitrary"), vmem_limit_bytes=VMEM_LIMIT),
    )(q, k, v, km, vm)


def _rank16_by_extraction(s):
    key = lax.broadcasted_iota(jnp.int32, s.shape, 0)
    rank = jnp.full(s.shape, float(PEER_TOPK), F32)
    for r in range(PEER_TOPK):
        m = jnp.max(s, axis=0, keepdims=True)
        first = jnp.min(jnp.where(s == m, key, N_KEYS), axis=0, keepdims=True)
        hit = key == first
        rank = jnp.where(hit, float(r), rank)
        s = jnp.where(hit, -jnp.inf, s)
    return rank


def _oddeven_merge(lo, hi, r):
    step = r * 2
    if step < hi - lo:
        yield from _oddeven_merge(lo, hi, step)
        yield from _oddeven_merge(lo + r, hi, step)
        yield from [(i, i + r) for i in range(lo + r, hi - r, step)]
    else:
        yield (lo, lo + r)


def _oddeven_merge_sort(lo, hi):
    if hi - lo >= 1:
        mid = lo + (hi - lo) // 2
        yield from _oddeven_merge_sort(lo, mid)
        yield from _oddeven_merge_sort(mid + 1, hi)
        yield from _oddeven_merge(lo, hi, 1)


_SORT16_NET = tuple(_oddeven_merge_sort(0, PEER_TOPK - 1))


def _sorted_top16(tiles):
    cols = list(tiles) + [None] * (PEER_TOPK - len(tiles))

    def vmax(x, y):
        return y if x is None else x if y is None else jnp.maximum(x, y)

    def vmin(x, y):
        return None if x is None or y is None else jnp.minimum(x, y)

    def exchange(i, j):
        cols[i], cols[j] = vmax(cols[i], cols[j]), vmin(cols[i], cols[j])

    for i, j in _SORT16_NET:
        exchange(i, j)
    for shift in (4, 2, 1):
        partner = [None if c is None else pltpu.roll(c, shift, axis=0) for c in reversed(cols)]
        cols = [vmax(cols[r], partner[r]) for r in range(PEER_TOPK)]
        for stride in (8, 4, 2, 1):
            for r in range(PEER_TOPK):
                if r & stride == 0:
                    exchange(r, r + stride)
    return cols


def _top16_tile(s, want_rank):
    nv = N_KEYS // SUBLANES
    rows = [s[SUBLANES * v:SUBLANES * (v + 1), :] for v in range(nv)]
    top = _sorted_top16(rows)
    vals = jnp.concatenate([t[0:1, :] for t in top], axis=0)
    at_least = jnp.zeros((SUBLANES, LANES), F32)
    for v in range(nv):
        at_least = at_least + jnp.where(rows[v] >= top[PEER_TOPK - 1], 1.0, 0.0)
    tied = jnp.where(jnp.sum(at_least, axis=0, keepdims=True) != float(PEER_TOPK), 1.0, 0.0)
    for r in range(PEER_TOPK - 1):
        tied = tied + jnp.where(top[r][0:1, :] <= top[r + 1][0:1, :], 1.0, 0.0)
    if not want_rank:
        return vals, None, tied
    rank = []
    for v in range(nv):
        rv = jnp.zeros((SUBLANES, LANES), F32)
        for r in range(PEER_TOPK):
            rv = rv + jnp.where(top[r] > rows[v], 1.0, 0.0)
        rank.append(rv)
    return vals, jnp.concatenate(rank, axis=0), tied


def _candidate_blocks():
    blocks = []
    for a in range(8):
        nb = PEER_TOPK // (a + 1)
        for b0 in range(0, nb, 8):
            nv = min(8, nb - b0)
            flat = np.array([a * PEER_TOPK + b0 + r for r in range(8)])
            blocks.append(('row', a, b0, nv, flat))
    flat = np.array([(8 + r) * PEER_TOPK for r in range(8)])
    blocks.append(('col', 8, 0, 8, flat))
    return blocks


_CAND_BLOCKS = _candidate_blocks()


def _select_counts(v1, v2, exact):
    sub = lax.broadcasted_iota(jnp.int32, (8, LANES), 0)
    vals = []
    for kind, a0, b0, nv, _ in _CAND_BLOCKS:
        if kind == 'row':
            blk = v1[a0:a0 + 1, :] + v2[b0:b0 + 8, :]
        else:
            blk = v1[a0:a0 + 8, :] + v2[b0:b0 + 1, :]
        if nv < 8:
            blk = jnp.where(sub < nv, blk, -jnp.inf)
        vals.append(blk)
    def by_pairwise_rank(blocks):
        beaten = [jnp.zeros((8, LANES), F32) for _ in blocks]
        for si, (_, _, _, nv_s, flat_s) in enumerate(_CAND_BLOCKS):
            for r in range(nv_s):
                src = jnp.broadcast_to(blocks[si][r:r + 1, :], (8, LANES))
                for ti, (_, _, _, _, flat_t) in enumerate(_CAND_BLOCKS):
                    wins_tie = flat_s[r] < flat_t
                    gt = jnp.where(src > blocks[ti], 1.0, 0.0)
                    ge = jnp.where(src >= blocks[ti], 1.0, 0.0)
                    if wins_tie.all():
                        beats = ge
                    elif not wins_tie.any():
                        beats = gt
                    else:
                        beats = jnp.where(sub >= int((~wins_tie).sum()), ge, gt)
                    beaten[ti] = beaten[ti] + beats
        return tuple(jnp.where(b < float(PEER_TOPK), 1.0, 0.0) for b in beaten)

    if exact:
        selected = by_pairwise_rank(vals)
        tied = jnp.zeros((1, LANES), F32)
    else:
        kth = _sorted_top16(vals)[PEER_TOPK - 1]
        selected = tuple(jnp.where(blk >= kth, 1.0, 0.0) for blk in vals)
        n_reach = jnp.sum(sum(selected[1:], selected[0]), axis=0, keepdims=True)
        tied = jnp.where(n_reach != float(PEER_TOPK), 1.0, 0.0)

    top = vals[0][0:1, :]
    z = jnp.zeros((1, LANES), F32)
    cnt_rows = [jnp.zeros((1, LANES), F32) for _ in range(PEER_TOPK)]
    for ti, (kind, a0, b0, nv, _) in enumerate(_CAND_BLOCKS):
        self_f = selected[ti]
        if nv < 8:
            self_f = jnp.where(sub < nv, self_f, 0.0)
        z = z + jnp.sum(self_f * jnp.exp(vals[ti] - top), axis=0, keepdims=True)
        if kind == 'row':
            cnt_rows[a0] = cnt_rows[a0] + jnp.sum(self_f, axis=0, keepdims=True)
        else:
            for r in range(8):
                cnt_rows[a0 + r] = cnt_rows[a0 + r] + self_f[r:r + 1, :]
    return jnp.concatenate(cnt_rows, axis=0), z, tied


def _mix_kernel(att_ref, cn_ref, x_ref, ag_ref, wo_ref, g2_ref, wqr_ref, k1_ref, k2_ref,
                h1_ref, xnt_ref, c_ref, e1_ref, r2_ref, e2_ref, st, rk, vs, *, tm):
    nlt = tm // LANES
    att = _rms(att_ref[...].astype(F32), ag_ref[...]).astype(BF16)
    d_att = att.shape[-1]
    h1 = x_ref[...] + _dot(att, wo_ref[0:d_att, :]) + _dot(cn_ref[...], wo_ref[d_att:, :])
    h1_ref[...] = h1
    xn32 = _rms(h1, g2_ref[...])
    xn = xn32.astype(BF16)
    xnt_ref[...] = xn32.T.astype(BF16)
    qp = _dot(xn, wqr_ref[...]).astype(BF16)
    half = D_KEY // 2
    for h in range(PEER_HEADS):
        for side, kref in enumerate((k1_ref, k2_ref)):
            lo = h * D_KEY + side * half
            sc = _dot_nt(kref[...], qp[:, lo:lo + half])
            for lt in range(nlt):
                st[2 * h + side, lt] = sc[:, lt * LANES:(lt + 1) * LANES]

    no_tie = jnp.zeros((1, LANES), F32)

    def rank_body(idx, tied):
        h = idx // nlt
        lt = idx % nlt
        vals1, _, t1 = _top16_tile(st[2 * h, lt], False)
        vals2, rank2, t2 = _top16_tile(st[2 * h + 1, lt], True)
        vs[2 * h, lt] = vals1
        vs[2 * h + 1, lt] = vals2
        rk[2 * h + 1, lt] = rank2
        return jnp.maximum(tied, jnp.maximum(t1, t2))

    def rank_exact_body(idx, carry):
        p = idx // nlt
        lt = idx % nlt
        rk[p, lt] = _rank16_by_extraction(st[p, lt])
        return carry

    def sel_body(exact, idx, tied):
        for hp in range(2):
            tied = sel_one(exact, 2 * (idx // (nlt // 2)) + hp, idx % (nlt // 2), tied)
        return tied

    def sel_one(exact, h, lt2, tied):
        for half in range(2):
            lt = 2 * lt2 + half
            ls = slice(half * LANES, (half + 1) * LANES)
            v1 = vs[2 * h, lt]
            v2 = vs[2 * h + 1, lt]
            s1 = st[2 * h, lt]
            cnt, z, t = _select_counts(v1, v2, exact)
            tied = jnp.maximum(tied, t)
            r1 = rk[2 * h, lt] if exact else None
            c = jnp.zeros((N_KEYS, LANES), F32)
            for a in range(PEER_TOPK):
                hit = (r1 == float(a)) if exact else (s1 == v1[a:a + 1, :])
                c = c + jnp.where(hit, cnt[a:a + 1, :], 0.0)
            c_ref[h, lt] = c
            e1_ref[h, lt] = jnp.exp(s1 - v1[0:1, :]) * (1.0 / z)
            r2_ref[h, lt2, :, ls] = rk[2 * h + 1, lt].astype(BF16)
            e2_ref[h, lt2, :, ls] = jnp.exp(st[2 * h + 1, lt] - v2[0:1, :]).astype(BF16)
        return tied

    tied = lax.fori_loop(0, PEER_HEADS * nlt, rank_body, no_tie)
    tied = lax.fori_loop(0, (PEER_HEADS // 2) * (nlt // 2), functools.partial(sel_body, False), tied)

    @pl.when(jnp.max(tied) > 0.0)
    def _():
        lax.fori_loop(0, 2 * PEER_HEADS * nlt, rank_exact_body, 0)
        lax.fori_loop(0, (PEER_HEADS // 2) * (nlt // 2), functools.partial(sel_body, True), no_tie)


def _mix_call(att, cn, x2, w, tm):
    n, d = x2.shape
    nlt = tm // LANES
    d_mix = w['w_out'].shape[0]
    const = lambda ti: (0, 0)
    tile4 = pl.BlockSpec((PEER_HEADS, nlt, N_KEYS, LANES), lambda ti: (0, ti, 0, 0))
    tile4p = pl.BlockSpec((PEER_HEADS, nlt // 2, N_KEYS, 2 * LANES), lambda ti: (0, ti, 0, 0))
    sel_shape = (PEER_HEADS, n // LANES, N_KEYS, LANES)
    selp_shape = (PEER_HEADS, n // (2 * LANES), N_KEYS, 2 * LANES)
    kern = functools.partial(_mix_kernel, tm=tm)
    return pl.pallas_call(
        kern,
        name="mix",
        grid=(n // tm,),
        in_specs=[
            pl.BlockSpec((tm, att.shape[-1]), lambda ti: (ti, 0)),
            pl.BlockSpec((tm, cn.shape[-1]), lambda ti: (ti, 0)),
            pl.BlockSpec((tm, d), lambda ti: (ti, 0)),
            pl.BlockSpec((1, att.shape[-1]), const),
            pl.BlockSpec((d_mix, d), const),
            pl.BlockSpec((1, d), const),
            pl.BlockSpec(w['w_query'].shape, const),
            pl.BlockSpec((N_KEYS, D_KEY // 2), const),
            pl.BlockSpec((N_KEYS, D_KEY // 2), const),
        ],
        out_specs=[
            pl.BlockSpec((tm, d), lambda ti: (ti, 0)),
            pl.BlockSpec((d, tm), lambda ti: (0, ti)),
            tile4, tile4, tile4p, tile4p,
        ],
        out_shape=[
            jax.ShapeDtypeStruct((n, d), F32),
            jax.ShapeDtypeStruct((d, n), BF16),
            jax.ShapeDtypeStruct(sel_shape, F32),
            jax.ShapeDtypeStruct(sel_shape, F32),
            jax.ShapeDtypeStruct(selp_shape, BF16),
            jax.ShapeDtypeStruct(selp_shape, BF16),
        ],
        scratch_shapes=[
            pltpu.VMEM((2 * PEER_HEADS, nlt, N_KEYS, LANES), F32),
            pltpu.VMEM((2 * PEER_HEADS, nlt, N_KEYS, LANES), F32),
            pltpu.VMEM((2 * PEER_HEADS, nlt, PEER_TOPK, LANES), F32),
        ],
        compiler_params=pltpu.CompilerParams(
            dimension_semantics=("arbitrary",), vmem_limit_bytes=VMEM_LIMIT),
    )(att, cn, x2, w['attn_out_g'], w['w_out'], w['norm2_g'], w['w_query'], w['keys1'], w['keys2'])


def _bf16_split(value):
    hi = float(np.asarray(value, dtype=BF16))
    lo = float(np.asarray(value - hi, dtype=BF16))
    return hi, lo


def _gelu_tanh(x):
    c = float(np.sqrt(2.0 / np.pi))
    c_hi, c_lo = _bf16_split(c)
    k_hi, k_lo = _bf16_split(c * 0.044715)
    x2 = x * x
    poly = (x2 * k_hi + c_hi) + (x2 * k_lo + c_lo)
    hx = 0.5 * x
    return hx + hx * jnp.tanh(x * poly)


def _peer_kernel(xnt_ref, u_ref, vt_ref, c_ref, e1_ref, r2_ref, e2_ref, h1_ref, fg_ref, y_ref,
                 acc, pbuf, tbl, *, t, ec):
    e = pl.program_id(1)
    nlt = t // LANES
    nib = ec // N_KEYS
    nsub = ec // SUB_EXPERTS
    kps = SUB_EXPERTS // N_KEYS
    npos = N_KEYS // PACK_ROWS

    @pl.when(e == 0)
    def _():
        acc[...] = jnp.zeros_like(acc)

    row0 = pl.multiple_of(e * nib, nib)
    for h in range(PEER_HEADS):
        for lt in range(nlt):
            ls = slice((lt % 2) * LANES, (lt % 2 + 1) * LANES)
            c8 = c_ref[h, lt, pl.ds(row0, nib), :]
            e8 = e1_ref[h, lt, pl.ds(row0, nib), :]
            for kk in range(nib):
                tbl[0, h, lt // 2, kk, :, ls] = jnp.broadcast_to(c8[kk:kk + 1, :], (PACK_ROWS, LANES)).astype(BF16)
                tbl[1, h, lt // 2, kk, :, ls] = jnp.broadcast_to(e8[kk:kk + 1, :], (PACK_ROWS, LANES)).astype(BF16)

    gates = {}
    for k in range(nib):
        for lt2 in range(nlt // 2):
            gate = None
            for h in range(PEER_HEADS):
                cb = jnp.concatenate([tbl[0, h, lt2, k]] * npos, axis=0)
                eb = jnp.concatenate([tbl[1, h, lt2, k]] * npos, axis=0)
                term = jnp.where(r2_ref[h, lt2] < cb, e2_ref[h, lt2] * eb, jnp.zeros((), BF16))
                gate = term if gate is None else gate + term
            gates[k, lt2] = gate
    kpa = A_ROWS // N_KEYS
    d_model = u_ref.shape[1]
    for j in range(ec // A_ROWS):
        lhs = u_ref[j * A_ROWS:(j + 1) * A_ROWS, :]
        if j > 0:
            anchor = gates[(j - 1) * kpa, 0][0:PACK_ROWS, 0:LANES]
            zero = jnp.minimum(jnp.abs(anchor), jnp.zeros((), BF16))
            zero = jnp.concatenate([jnp.concatenate([zero] * (d_model // LANES), axis=1)] * (A_ROWS // PACK_ROWS),
                                   axis=0)
            lhs = lhs + zero
        act = _dot(lhs, xnt_ref[...])
        for kk in range(kpa):
            k = j * kpa + kk
            for lt2 in range(nlt // 2):
                ls = slice(lt2 * 2 * LANES, (lt2 + 1) * 2 * LANES)
                gl = _gelu_tanh(act[kk * N_KEYS:(kk + 1) * N_KEYS, ls].astype(BF16))
                pbuf[0, k * N_KEYS:(k + 1) * N_KEYS, ls] = gates[k, lt2] * gl
    acc[...] += _dot(vt_ref[0], pbuf[0])

    @pl.when(e == pl.num_programs(1) - 1)
    def _():
        h2 = h1_ref[...] + acc[...].T
        y_ref[...] = _rms(h2, fg_ref[...])


def _peer_call(xnt, u, vt, c, e1, r2, e2, h1, fg, t, ec):
    d, n = xnt.shape
    ne = u.shape[0] // ec
    nsub = ec // SUB_EXPERTS
    assert vt.shape == (ne * nsub, d, SUB_EXPERTS)
    nlt = t // LANES
    tile4 = pl.BlockSpec((PEER_HEADS, nlt, N_KEYS, LANES), lambda ti, ei: (0, ti, 0, 0))
    tile4p = pl.BlockSpec((PEER_HEADS, nlt // 2, N_KEYS, 2 * LANES), lambda ti, ei: (0, ti, 0, 0))
    kern = functools.partial(_peer_kernel, t=t, ec=ec)
    return pl.pallas_call(
        kern,
        name="peer",
        grid=(n // t, ne),
        in_specs=[
            pl.BlockSpec((d, t), lambda ti, ei: (0, ti)),
            pl.BlockSpec((ec, d), lambda ti, ei: (ei, 0)),
            pl.BlockSpec((nsub, d, SUB_EXPERTS), lambda ti, ei: (ei, 0, 0)),
            tile4, tile4, tile4p, tile4p,
            pl.BlockSpec((t, d), lambda ti, ei: (ti, 0)),
            pl.BlockSpec((1, d), lambda ti, ei: (0, 0)),
        ],
        out_specs=pl.BlockSpec((t, d), lambda ti, ei: (ti, 0)),
        out_shape=jax.ShapeDtypeStruct((n, d), F32),
        scratch_shapes=[pltpu.VMEM((d, t), F32), pltpu.VMEM((nsub, SUB_EXPERTS, t), BF16),
                        pltpu.VMEM((2, PEER_HEADS, nlt // 2, ec // N_KEYS, PACK_ROWS, 2 * LANES), BF16)],
        compiler_params=pltpu.CompilerParams(
            dimension_semantics=("arbitrary", "arbitrary"), vmem_limit_bytes=VMEM_LIMIT),
    )(xnt, u, vt, c, e1, r2, e2, h1, fg)


def _rope_slab_tables(length):
    inv = 1.0 / (ROPE_THETA ** (jnp.arange(0, QK_ROPE, 2, dtype=F32) / QK_ROPE))
    ang = jnp.arange(length, dtype=F32)[:, None] * inv[None, :]
    cos, sin = jnp.cos(ang), jnp.sin(ang)
    half = QK_ROPE // 2
    pad = HEAD_SLAB - QK_NOPE - QK_ROPE
    ones = jnp.ones((length, QK_NOPE), F32)
    zeros_n = jnp.zeros((length, QK_NOPE), F32)
    zeros_p = jnp.zeros((length, pad), F32)
    del half
    cos_t = jnp.concatenate([ones, cos, cos, zeros_p], axis=1)
    sin_t = jnp.concatenate([zeros_n, -sin, sin, zeros_p], axis=1)
    return cos_t, sin_t


def _head_slabs(wcols, n_heads, width, offset=0):
    k = wcols.shape[0]
    w3 = wcols.reshape(k, n_heads, width)

    def at(off):
        return jnp.pad(w3, ((0, 0), (0, 0), (off, HEAD_SLAB - width - off)))

    if isinstance(offset, int):
        return at(offset).reshape(k, n_heads * HEAD_SLAB)
    out = None
    for off in sorted(set(offset)):
        pick = jnp.asarray([o == off for o in offset])[None, :, None]
        out = jnp.where(pick, at(off), 0.0) if out is None else jnp.where(pick, at(off), out)
    return out.reshape(k, n_heads * HEAD_SLAB)


def _prep_weights(norm1_g, w_in, q_norm_g, w_uq, kv_norm_g, w_ukv, conv_w, conv_b, attn_out_g,
                  conv_out_g, w_out, norm2_g, peer_w_query, peer_keys1, peer_keys2):
    half = QK_ROPE // 2
    o1 = Q_LORA
    o2 = o1 + KV_LORA
    o3 = o2 + QK_ROPE
    k_in = w_in.shape[0]
    kr = w_in[:, o2:o3]
    zpad_n = jnp.zeros((k_in, QK_NOPE), F32)
    zpad_p = jnp.zeros((k_in, HEAD_SLAB - QK_NOPE - QK_ROPE), F32)
    kr_a = jnp.concatenate([zpad_n, kr[:, :half], kr[:, half:], zpad_p], axis=1)
    kr_b = jnp.concatenate([zpad_n, kr[:, half:], kr[:, :half], zpad_p], axis=1)
    w_in_ext = jnp.concatenate([w_in[:, :o2], kr_a, kr_b, w_in[:, o3:]], axis=1).astype(BF16)

    qd = QK_NOPE + QK_ROPE
    wq3 = w_uq.reshape(Q_LORA, ATT_HEADS, qd)
    wq_swap = jnp.concatenate([wq3[:, :, :QK_NOPE], wq3[:, :, QK_NOPE + half:], wq3[:, :, QK_NOPE:QK_NOPE + half]],
                              axis=2).reshape(Q_LORA, ATT_HEADS * qd)
    w_q = jnp.concatenate([_head_slabs(w_uq, ATT_HEADS, qd), _head_slabs(wq_swap, ATT_HEADS, qd)],
                          axis=1).astype(BF16)

    wkv3 = w_ukv.reshape(KV_LORA, ATT_HEADS, QK_NOPE + V_HEAD)
    wk = wkv3[:, :, :QK_NOPE].reshape(KV_LORA, ATT_HEADS * QK_NOPE)
    wv = wkv3[:, :, QK_NOPE:].reshape(KV_LORA, ATT_HEADS * V_HEAD)
    v_off = [_v_lane(h) for h in range(ATT_HEADS)]
    w_kv = jnp.concatenate([_head_slabs(wk, ATT_HEADS, QK_NOPE), _head_slabs(wv, ATT_HEADS, V_HEAD, v_off)],
                           axis=1).astype(BF16)
    one_cols = np.zeros((1, ATT_HEADS * HEAD_SLAB), np.float32)
    for h in range(ATT_HEADS):
        one_cols[0, h * HEAD_SLAB + _one_lane(h)] = 1.0
    return {
        'v_one': jnp.asarray(one_cols),
        'norm1_g': norm1_g.reshape(1, -1), 'w_in': w_in_ext,
        'q_norm_g': q_norm_g.reshape(1, -1), 'w_q': w_q,
        'kv_norm_g': kv_norm_g.reshape(1, -1), 'w_kv': w_kv,
        'conv_w': conv_w, 'conv_b': conv_b.reshape(1, -1), 'conv_out_g': conv_out_g.reshape(1, -1),
        'attn_out_g': attn_out_g.reshape(1, -1), 'w_out': w_out.astype(BF16),
        'norm2_g': norm2_g.reshape(1, -1), 'w_query': peer_w_query.astype(BF16),
        'keys1': peer_keys1.astype(BF16), 'keys2': peer_keys2.astype(BF16),
    }


def kernel(x, meta_tokens, norm1_g, w_in, q_norm_g, w_uq, kv_norm_g, w_ukv, conv_w, conv_b, attn_out_g,
           conv_out_g, w_out, norm2_g, peer_w_query, peer_keys1, peer_keys2, peer_u, peer_v, final_g):
    b, s, d = x.shape
    assert norm1_g.shape[0] == 1, "single-layer kernel"
    assert meta_tokens.shape[0] == N_META
    tm = min(TM_PROJ, s)
    assert s % tm == 0 and s % LANES == 0
    w = _prep_weights(norm1_g[0], w_in[0], q_norm_g[0], w_uq[0], kv_norm_g[0], w_ukv[0], conv_w[0],
                      conv_b[0], attn_out_g[0], conv_out_g[0], w_out[0], norm2_g[0], peer_w_query[0],
                      peer_keys1[0], peer_keys2[0])
    d_conv = conv_b.shape[-1]
    cos_t, sin_t = _rope_slab_tables(N_META + s)

    _, k_m, v_m, _, z_m = _proj_call(meta_tokens[None], jnp.zeros((8, d_conv), F32), w,
                                     cos_t[:N_META], sin_t[:N_META], N_META)
    q, k, v, cn, _ = _proj_call(x, z_m[0], w, cos_t[N_META:], sin_t[N_META:], tm)
    hw = ATT_HEADS * HEAD_SLAB
    km = jnp.pad(k_m[0], ((0, LANES - N_META), (0, 0)))
    vm = jnp.pad(v_m[0], ((0, LANES - N_META), (0, 0)))
    att = _attn_call(q, k, v, km, vm, min(TQ_ATT, s), HEADS_PER_STEP)

    n = b * s
    tmix = min(TM_MIX, n)
    h1, xnt, c, e1, r2, e2 = _mix_call(att.reshape(n, D_ATT), cn.reshape(n, d_conv), x.reshape(n, d), w, tmix)
    u_bf = peer_u[0].astype(BF16)
    vt_bf = peer_v[0].reshape(-1, SUB_EXPERTS, d).transpose(0, 2, 1).astype(BF16)
    y = _peer_call(xnt, u_bf, vt_bf, c, e1, r2, e2, h1, final_g.reshape(1, -1), min(T_PEER, n), E_CHUNK)
    return y.reshape(b, s, d)
```
